```python
import math
import jax, jax.numpy as jnp
from jax import lax
import numpy as np

D_MODEL = 1024
BATCH = 4
SEQ = 8192
DEPTH = 2

DN_ALPHA = (2 * DEPTH) ** 0.25
DN_BETA = (8 * DEPTH) ** -0.25
N_EVEN = (DEPTH + 1) // 2
N_ODD = DEPTH // 2

CHUNK = 64
CONV_K = 4
LN_EPS = 1e-5
RMS_EPS = 1e-6
D_FF = 256 * ((8 * D_MODEL // 3 + 255) // 256)

GDN_HEADS = 8
GDN_DK = 64
GDN_DV = 64
MLSTM_HEADS = 4
MLSTM_DK = 64
MLSTM_DV = 128
HYB_WIDTH = GDN_HEADS * GDN_DV + MLSTM_HEADS * MLSTM_DV
GDN_CONV_DIM = 2 * GDN_HEADS * GDN_DK + GDN_HEADS * GDN_DV
HYB_SPLITS = (GDN_HEADS * GDN_DK, GDN_HEADS * GDN_DK, GDN_HEADS * GDN_DV, GDN_HEADS * GDN_DV, GDN_HEADS, GDN_HEADS,
              MLSTM_HEADS * MLSTM_DK, MLSTM_HEADS * MLSTM_DK, MLSTM_HEADS * MLSTM_DV, MLSTM_HEADS * MLSTM_DV,
              MLSTM_HEADS, MLSTM_HEADS)
HYB_IN = sum(HYB_SPLITS)

SSD_D_INNER = 2 * D_MODEL
SSD_HEADDIM = 64
SSD_HEADS = SSD_D_INNER // SSD_HEADDIM
SSD_GROUPS = 4
SSD_HPG = SSD_HEADS // SSD_GROUPS
SSD_STATE = 128
SSD_CONV_DIM = SSD_D_INNER + 2 * SSD_GROUPS * SSD_STATE
SSD_IN = SSD_D_INNER + SSD_CONV_DIM + SSD_HEADS

kernel_name = 'hybrid_gdn_mlstm_ssd_macaron_deepnorm'


def _split(t, sizes):
    return jnp.split(t, np.cumsum(sizes)[:-1].tolist(), axis=-1)


def _layer_norm(x, g, b):
    xf = x.astype(jnp.float32)
    xc = xf - jnp.mean(xf, -1, keepdims=True)
    var = jnp.mean(xc * xc, -1, keepdims=True)
    return (xc * lax.rsqrt(var + LN_EPS) * g.astype(jnp.float32) + b.astype(jnp.float32)).astype(x.dtype)


def _rms_norm(x, w):
    xf = x.astype(jnp.float32)
    return xf * lax.rsqrt(jnp.mean(xf * xf, -1, keepdims=True) + RMS_EPS) * w.astype(jnp.float32)


def _l2_normalize(t):
    return t * lax.rsqrt(jnp.sum(t * t, -1, keepdims=True) + 1e-6)


def _causal_dwconv(x, w):
    return lax.conv_general_dilated(x, w[:, None, :].astype(x.dtype), window_strides=(1,),
                                    padding=[(CONV_K - 1, 0)], dimension_numbers=('NWC', 'WIO', 'NWC'),
                                    feature_group_count=x.shape[-1])


def _swiglu(x, w_gate, w_up, w_down):
    return (jax.nn.silu(x @ w_gate) * (x @ w_up)) @ w_down


def _to_chunks(t):
    b, s = t.shape[:2]
    t = t.reshape(b, s // CHUNK, CHUNK, *t.shape[2:])
    return jnp.moveaxis(t, 3, 1)


def _from_chunks(t):
    t = jnp.moveaxis(t, 1, 3)
    return t.reshape(t.shape[0], t.shape[1] * t.shape[2], *t.shape[3:])


def _gated_delta_rule(q, k, v, g, beta):
    bsz, s, nh, dk = q.shape
    dv = v.shape[-1]
    q, k, v, g, beta = map(_to_chunks, (q, k, v, g, beta))
    causal = jnp.tril(jnp.ones((CHUNK, CHUNK), bool))
    strict = jnp.tril(jnp.ones((CHUNK, CHUNK), bool), -1)
    gc = jnp.cumsum(g, axis=-1)
    decay = jnp.exp(jnp.where(causal, gc[..., :, None] - gc[..., None, :], -jnp.inf))
    kb = k * beta[..., None]
    lower = jnp.where(strict, jnp.einsum('bhncd,bhnjd->bhncj', kb, k) * decay, 0.0)
    rhs = jnp.concatenate([v * beta[..., None], kb * jnp.exp(gc)[..., None]], axis=-1)
    sol = lax.linalg.triangular_solve(lower + jnp.eye(CHUNK, dtype=lower.dtype), rhs, left_side=True, lower=True)
    u_base, w_dec = sol[..., :dv], sol[..., dv:]
    attn = jnp.einsum('bhncd,bhnjd->bhncj', q, k) * decay
    q_dec = q * jnp.exp(gc)[..., None]
    k_dec = k * jnp.exp(gc[..., -1:] - gc)[..., None]
    g_last = jnp.exp(gc[..., -1])
    xs = tuple(jnp.moveaxis(t, 2, 0) for t in (u_base, w_dec, attn, q_dec, k_dec, g_last))

    def step(state, inp):
        u_b, w_c, a_c, q_c, k_c, gl_c = inp
        u = u_b - jnp.einsum('bhcd,bhde->bhce', w_c, state)
        o = jnp.einsum('bhcd,bhde->bhce', q_c, state) + jnp.einsum('bhcj,bhje->bhce', a_c, u)
        state = gl_c[..., None, None] * state + jnp.einsum('bhcd,bhce->bhde', k_c, u)
        return state, o

    _, o = lax.scan(step, jnp.zeros((bsz, nh, dk, dv), q.dtype), xs)
    return _from_chunks(jnp.moveaxis(o, 0, 2))


def _mlstm_chunkwise(q, k, v, log_i, log_f):
    bsz, s, nh, dk = q.shape
    dv = v.shape[-1]
    q, k, v, log_i, log_f = map(_to_chunks, (q, k, v, log_i, log_f))
    causal = jnp.tril(jnp.ones((CHUNK, CHUNK), bool))
    b = jnp.cumsum(log_f, axis=-1)
    d_log = jnp.where(causal, b[..., :, None] - b[..., None, :] + log_i[..., None, :], -jnp.inf)
    m_intra = jnp.max(d_log, axis=-1)
    s_qk = jnp.einsum('bhncd,bhnjd->bhncj', q, k) * jnp.exp(d_log - m_intra[..., None])
    num_intra = jnp.einsum('bhncj,bhnje->bhnce', s_qk, v)
    den_intra = jnp.sum(s_qk, axis=-1)
    w_end = b[..., -1:] - b + log_i
    m_end = jnp.max(w_end, axis=-1)
    k_end = k * jnp.exp(w_end - m_end[..., None])[..., None]
    c_end = jnp.einsum('bhncd,bhnce->bhnde', k_end, v)
    n_end = jnp.sum(k_end, axis=-2)
    b_last = b[..., -1]
    xs = tuple(jnp.moveaxis(t, 2, 0) for t in (q, b, m_intra, num_intra, den_intra, b_last, m_end, c_end, n_end))

    def step(carry, inp):
        c_st, n_st, m_st = carry
        q_c, b_c, mi_c, num_c, den_c, bl_c, me_c, ce_c, ne_c = inp
        a = b_c + m_st[..., None]
        m_t = jnp.maximum(a, mi_c)
        ea, ei = jnp.exp(a - m_t), jnp.exp(mi_c - m_t)
        num = ea[..., None] * jnp.einsum('bhcd,bhde->bhce', q_c, c_st) + ei[..., None] * num_c
        den = ea * jnp.einsum('bhcd,bhd->bhc', q_c, n_st) + ei * den_c
        h_c = num / jnp.maximum(jnp.abs(den), jnp.exp(-m_t))[..., None]
        m_new = jnp.maximum(bl_c + m_st, me_c)
        sc, se = jnp.exp(bl_c + m_st - m_new), jnp.exp(me_c - m_new)
        c_st = sc[..., None, None] * c_st + se[..., None, None] * ce_c
        n_st = sc[..., None] * n_st + se[..., None] * ne_c
        return (c_st, n_st, m_new), h_c

    init = (jnp.zeros((bsz, nh, dk, dv), q.dtype), jnp.zeros((bsz, nh, dk), q.dtype), jnp.zeros((bsz, nh), q.dtype))
    _, hs = lax.scan(step, init, xs)
    return _from_chunks(jnp.moveaxis(hs, 0, 2))


def _ssd_chunked(xh, dt, a, bm, cm):
    bsz, s = xh.shape[:2]
    n = s // CHUNK
    xh = xh.reshape(bsz, n, CHUNK, *xh.shape[2:])
    dt = dt.reshape(bsz, n, CHUNK, *dt.shape[2:])
    bm = bm.reshape(bsz, n, CHUNK, *bm.shape[2:])
    cm = cm.reshape(bsz, n, CHUNK, *cm.shape[2:])
    causal = jnp.tril(jnp.ones((CHUNK, CHUNK), bool))
    a_cum = jnp.cumsum(jnp.moveaxis(dt * a, 2, -1), axis=-1)
    xdt = xh * dt[..., None]
    cb = jnp.einsum('bncgs,bnjgs->bngcj', cm, bm)
    seg = jnp.exp(jnp.where(causal, a_cum[..., :, None] - a_cum[..., None, :], -jnp.inf))
    y_diag = jnp.einsum('bngcj,bngrcj,bnjgrp->bncgrp', cb, seg, xdt)
    e_cum = jnp.exp(a_cum)
    d_end = jnp.exp(a_cum[..., -1:] - a_cum)
    g_last = e_cum[..., -1]
    xs = tuple(jnp.moveaxis(t, 1, 0) for t in (cm, bm, xdt, e_cum, d_end, g_last))

    def step(h, inp):
        c_c, b_c, x_c, ec_c, de_c, gl_c = inp
        y_off = jnp.einsum('bcgs,bgrps,bgrc->bcgrp', c_c, h, ec_c)
        h = gl_c[..., None, None] * h + jnp.einsum('bjgs,bgrj,bjgrp->bgrps', b_c, de_c, x_c)
        return h, y_off

    h0 = jnp.zeros((bsz, SSD_GROUPS, SSD_HPG, SSD_HEADDIM, SSD_STATE), xh.dtype)
    _, y_off = lax.scan(step, h0, xs)
    y = y_diag + jnp.moveaxis(y_off, 0, 1)
    return y.reshape(bsz, s, SSD_GROUPS, SSD_HPG, SSD_HEADDIM)


def _hybrid_mixer(x, w_in, conv_w, a_log, dt_bias, norm_w, i_bias, f_bias, w_out):
    bsz, s, _ = x.shape
    f32 = jnp.float32
    gq, gk, gv, gz, gb, ga, mq, mk, mv, mo, mi, mf = _split(x @ w_in, HYB_SPLITS)
    qkv = jax.nn.silu(_causal_dwconv(jnp.concatenate([gq, gk, gv], -1), conv_w)).astype(f32)
    gq, gk, gv = _split(qkv, (GDN_HEADS * GDN_DK, GDN_HEADS * GDN_DK, GDN_HEADS * GDN_DV))
    gq = _l2_normalize(gq.reshape(bsz, s, GDN_HEADS, GDN_DK)) * GDN_DK ** -0.5
    gk = _l2_normalize(gk.reshape(bsz, s, GDN_HEADS, GDN_DK))
    gv = gv.reshape(bsz, s, GDN_HEADS, GDN_DV)
    beta = jax.nn.sigmoid(gb.astype(f32))
    g = -jnp.exp(a_log.astype(f32)) * jax.nn.softplus(ga.astype(f32) + dt_bias.astype(f32))
    o_a = _gated_delta_rule(gq, gk, gv, g, beta)
    o_a = _rms_norm(o_a, norm_w) * jax.nn.silu(gz.astype(f32)).reshape(bsz, s, GDN_HEADS, GDN_DV)
    mq = mq.astype(f32).reshape(bsz, s, MLSTM_HEADS, MLSTM_DK)
    mk = mk.astype(f32).reshape(bsz, s, MLSTM_HEADS, MLSTM_DK) * MLSTM_DK ** -0.5
    mv = mv.astype(f32).reshape(bsz, s, MLSTM_HEADS, MLSTM_DV)
    log_i = mi.astype(f32) + i_bias.astype(f32)
    log_f = jax.nn.log_sigmoid(mf.astype(f32) + f_bias.astype(f32))
    h = _mlstm_chunkwise(mq, mk, mv, log_i, log_f)
    o_b = jax.nn.sigmoid(mo.astype(f32)).reshape(bsz, s, MLSTM_HEADS, MLSTM_DV) * h
    y = jnp.concatenate([o_a.reshape(bsz, s, -1), o_b.reshape(bsz, s, -1)], axis=-1).astype(x.dtype)
    return y @ w_out


def _ssd_mixer(x, w_in, conv_w, conv_b, a_log, dt_bias, d_skip, norm_w, w_out):
    bsz, s, _ = x.shape
    f32 = jnp.float32
    z, xbc, dt = _split(x @ w_in, (SSD_D_INNER, SSD_CONV_DIM, SSD_HEADS))
    xbc = jax.nn.silu(_causal_dwconv(xbc, conv_w) + conv_b).astype(f32)
    xs, bm, cm = _split(xbc, (SSD_D_INNER, SSD_GROUPS * SSD_STATE, SSD_GROUPS * SSD_STATE))
    xs = xs.reshape(bsz, s, SSD_GROUPS, SSD_HPG, SSD_HEADDIM)
    dt = jax.nn.softplus(dt.astype(f32) + dt_bias.astype(f32)).reshape(bsz, s, SSD_GROUPS, SSD_HPG)
    a = -jnp.exp(a_log.astype(f32)).reshape(SSD_GROUPS, SSD_HPG)
    y = _ssd_chunked(xs, dt, a, bm.reshape(bsz, s, SSD_GROUPS, SSD_STATE), cm.reshape(bsz, s, SSD_GROUPS, SSD_STATE))
    y = y + d_skip.astype(f32).reshape(SSD_GROUPS, SSD_HPG)[..., None] * xs
    y = y.reshape(bsz, s, SSD_D_INNER) * jax.nn.silu(z.astype(f32))
    y = _rms_norm(y.reshape(bsz, s, SSD_GROUPS, SSD_D_INNER // SSD_GROUPS), norm_w.reshape(SSD_GROUPS, -1))
    return y.reshape(bsz, s, SSD_D_INNER).astype(x.dtype) @ w_out


def setup_inputs(seed: int = 0) -> dict:
    key = jax.random.key(seed)
    keys = iter(jax.random.split(key, 64))

    def nrm(shape, scale):
        return scale * jax.random.normal(next(keys), shape, jnp.float32)

    def unif(shape, lo, hi):
        return jax.random.uniform(next(keys), shape, jnp.float32, lo, hi)

    def gain(shape):
        return 1.0 + nrm(shape, 0.02)

    def dt_bias(shape):
        dt = jnp.exp(unif(shape, math.log(1e-3), math.log(1e-1)))
        return dt + jnp.log(-jnp.expm1(-dt))

    return {
        'x': nrm((BATCH, SEQ, D_MODEL), 1.0),
        'ffn_pre_w_gate': nrm((DEPTH, D_MODEL, D_FF), D_MODEL ** -0.5),
        'ffn_pre_w_up': nrm((DEPTH, D_MODEL, D_FF), D_MODEL ** -0.5),
        'ffn_pre_w_down': nrm((DEPTH, D_FF, D_MODEL), DN_BETA * D_FF ** -0.5),
        'ln_pre_g': gain((DEPTH, D_MODEL)),
        'ln_pre_b': nrm((DEPTH, D_MODEL), 0.02),
        'hyb_w_in': nrm((N_EVEN, D_MODEL, HYB_IN), D_MODEL ** -0.5),
        'hyb_conv_w': nrm((N_EVEN, CONV_K, GDN_CONV_DIM), CONV_K ** -0.5),
        'gdn_a_log': jnp.log(unif((N_EVEN, GDN_HEADS), 1.0, 16.0)),
        'gdn_dt_bias': dt_bias((N_EVEN, GDN_HEADS)),
        'gdn_norm_w': gain((N_EVEN, GDN_DV)),
        'mlstm_i_bias': nrm((N_EVEN, MLSTM_HEADS), 0.1),
        'mlstm_f_bias': jnp.linspace(3.0, 6.0, MLSTM_HEADS)[None, :] + nrm((N_EVEN, MLSTM_HEADS), 0.1),
        'hyb_w_out': nrm((N_EVEN, HYB_WIDTH, D_MODEL), DN_BETA * HYB_WIDTH ** -0.5),
        'ssd_w_in': nrm((N_ODD, D_MODEL, SSD_IN), D_MODEL ** -0.5),
        'ssd_conv_w': nrm((N_ODD, CONV_K, SSD_CONV_DIM), CONV_K ** -0.5),
        'ssd_conv_b': nrm((N_ODD, SSD_CONV_DIM), 0.02),
        'ssd_a_log': jnp.log(unif((N_ODD, SSD_HEADS), 1.0, 16.0)),
        'ssd_dt_bias': dt_bias((N_ODD, SSD_HEADS)),
        'ssd_d_skip': gain((N_ODD, SSD_HEADS)),
        'ssd_norm_w': gain((N_ODD, SSD_D_INNER)),
        'ssd_w_out': nrm((N_ODD, SSD_D_INNER, D_MODEL), DN_BETA * SSD_D_INNER ** -0.5),
        'ln_mix_g': gain((DEPTH, D_MODEL)),
        'ln_mix_b': nrm((DEPTH, D_MODEL), 0.02),
        'ffn_post_w_gate': nrm((DEPTH, D_MODEL, D_FF), D_MODEL ** -0.5),
        'ffn_post_w_up': nrm((DEPTH, D_MODEL, D_FF), D_MODEL ** -0.5),
        'ffn_post_w_down': nrm((DEPTH, D_FF, D_MODEL), DN_BETA * D_FF ** -0.5),
        'ln_post_g': gain((DEPTH, D_MODEL)),
        'ln_post_b': nrm((DEPTH, D_MODEL), 0.02),
    }


def reference(x, ffn_pre_w_gate, ffn_pre_w_up, ffn_pre_w_down, ln_pre_g, ln_pre_b,
              hyb_w_in, hyb_conv_w, gdn_a_log, gdn_dt_bias, gdn_norm_w, mlstm_i_bias, mlstm_f_bias, hyb_w_out,
              ssd_w_in, ssd_conv_w, ssd_conv_b, ssd_a_log, ssd_dt_bias, ssd_d_skip, ssd_norm_w, ssd_w_out,
              ln_mix_g, ln_mix_b, ffn_post_w_gate, ffn_post_w_up, ffn_post_w_down, ln_post_g, ln_post_b):
    for l in range(DEPTH):
        x = _layer_norm(DN_ALPHA * x + 0.5 * _swiglu(x, ffn_pre_w_gate[l], ffn_pre_w_up[l], ffn_pre_w_down[l]),
                        ln_pre_g[l], ln_pre_b[l])
        j = l // 2
        if l % 2 == 0:
            mix = _hybrid_mixer(x, hyb_w_in[j], hyb_conv_w[j], gdn_a_log[j], gdn_dt_bias[j], gdn_norm_w[j],
                                mlstm_i_bias[j], mlstm_f_bias[j], hyb_w_out[j])
        else:
            mix = _ssd_mixer(x, ssd_w_in[j], ssd_conv_w[j], ssd_conv_b[j], ssd_a_log[j], ssd_dt_bias[j],
                             ssd_d_skip[j], ssd_norm_w[j], ssd_w_out[j])
        x = _layer_norm(DN_ALPHA * x + mix, ln_mix_g[l], ln_mix_b[l])
        x = _layer_norm(DN_ALPHA * x + 0.5 * _swiglu(x, ffn_post_w_gate[l], ffn_post_w_up[l], ffn_post_w_down[l]),
                        ln_post_g[l], ln_post_b[l])
    return x
```

```python
import functools

import jax
import jax.numpy as jnp
import numpy as np
from jax import lax
from jax.experimental import pallas as pl
from jax.experimental.pallas import tpu as pltpu

F32 = jnp.float32
BF16 = jnp.bfloat16

LN_EPS = 1e-5
RMS_EPS = 1e-6
L2_EPS = 1e-6
CHUNK = 64
CONV_K = 4
NEG = -1e30

GDN_HEADS, GDN_DK, GDN_DV = 8, 64, 64
MLSTM_HEADS, MLSTM_DK, MLSTM_DV = 4, 64, 128
SSD_HEADDIM, SSD_GROUPS, SSD_STATE = 64, 4, 128

LANES = 128
SUBLANES = 8
VMEM_LIMIT = 56 * 1024 * 1024
GROUP = 4 * CHUNK

TM_DENSE = 512
TT_MIX = 256


def _cparams(*sem):
    return pltpu.CompilerParams(dimension_semantics=sem, vmem_limit_bytes=VMEM_LIMIT)


def _dot(a, b):
    return jnp.dot(a, b, preferred_element_type=F32)


def _dot_nt(a, b):
    return lax.dot_general(a, b, (((1,), (1,)), ((), ())), preferred_element_type=F32)


def _dot_tn(a, b):
    return lax.dot_general(a, b, (((0,), (0,)), ((), ())), preferred_element_type=F32)


def _layer_norm(y, g, b):
    mu = jnp.mean(y, -1, keepdims=True)
    yc = y - mu
    var = jnp.mean(yc * yc, -1, keepdims=True)
    return yc * lax.rsqrt(var + LN_EPS) * g + b


def _sigmoid(x):
    return 1.0 / (1.0 + jnp.exp(-x))


def _silu(x):
    return x * _sigmoid(x)


def _softplus(x):
    return jnp.maximum(x, 0.0) + jnp.log(1.0 + jnp.exp(-jnp.abs(x)))


def _split(x, parts):
    out = []
    r = x
    for p in range(parts):
        h = r.astype(BF16)
        out.append(h)
        if p + 1 < parts:
            r = r - h.astype(F32)
    return out


def _sel_rows(m01x3_ref, x):
    return _dot(m01x3_ref[...], jnp.concatenate(_split(x, 3), axis=0))


def _sel_cols(x, m01x3_ref, parts=3):
    return _dot(jnp.concatenate(_split(x, parts), axis=1), m01x3_ref[...])


def _tile4(x):
    return jnp.concatenate([x, x, x, x], axis=0)


def _block_diag(x, mask):
    return (_tile4(x) * mask).astype(BF16)


def _ffn_kernel(x_ref, wg_ref, wu_ref, wd_ref, g_ref, b_ref, o_ref, *, fc, alpha):
    x = x_ref[...]
    xb = x.astype(BF16)
    acc = jnp.zeros_like(x)
    for c in range(wg_ref.shape[1] // fc):
        sl = slice(c * fc, (c + 1) * fc)
        hg = _dot(xb, wg_ref[:, sl])
        hu = _dot(xb, wu_ref[:, sl])
        acc = acc + _dot((_silu(hg) * hu).astype(BF16), wd_ref[sl, :])
    o_ref[...] = _layer_norm(alpha * x + 0.5 * acc, g_ref[...], b_ref[...])


def _const_spec(shape, grid_rank=1):
    zeros = (0,) * len(shape)
    if grid_rank == 1:
        return pl.BlockSpec(shape, lambda i: zeros, pipeline_mode=pl.Buffered(1))
    return pl.BlockSpec(shape, lambda b, i: zeros, pipeline_mode=pl.Buffered(1))


def _ffn(x, wg, wu, wd, g, b, alpha):
    t, d = x.shape
    f = wg.shape[1]
    tm = TM_DENSE
    return pl.pallas_call(
        functools.partial(_ffn_kernel, fc=2 * LANES, alpha=alpha),
        grid=(t // tm,),
        in_specs=[pl.BlockSpec((tm, d), lambda i: (i, 0)),
                  _const_spec((d, f)), _const_spec((d, f)), _const_spec((f, d)),
                  _const_spec((1, d)), _const_spec((1, d))],
        out_specs=pl.BlockSpec((tm, d), lambda i: (i, 0)),
        out_shape=jax.ShapeDtypeStruct((t, d), F32),
        compiler_params=_cparams("parallel"),
        name="ffn",
    )(x, wg.astype(BF16), wu.astype(BF16), wd.astype(BF16), g.reshape(1, d), b.reshape(1, d))


def _proj_kernel(x_ref, w_ref, o_ref, *, tn):
    xb = x_ref[...].astype(BF16)
    n = w_ref.shape[1]
    for start in range(0, n, tn):
        sl = slice(start, min(start + tn, n))
        o_ref[:, sl] = _dot(xb, w_ref[:, sl])


def _proj(x, w):
    t, d = x.shape
    n = w.shape[1]
    tm = TM_DENSE
    return pl.pallas_call(
        functools.partial(_proj_kernel, tn=4 * LANES),
        grid=(t // tm,),
        in_specs=[pl.BlockSpec((tm, d), lambda i: (i, 0)), _const_spec((d, n))],
        out_specs=pl.BlockSpec((tm, n), lambda i: (i, 0)),
        out_shape=jax.ShapeDtypeStruct((t, n), F32),
        compiler_params=_cparams("parallel"),
        name="in_proj",
    )(x, w.astype(BF16))


def _out_ln_kernel(*refs, n_in, alpha):
    y_refs, w_refs = refs[:n_in], refs[n_in:2 * n_in]
    x_ref, g_ref, b_ref, o_ref = refs[2 * n_in:]
    acc = alpha * x_ref[...]
    for y_ref, w_ref in zip(y_refs, w_refs):
        acc = acc + _dot(y_ref[...].astype(BF16), w_ref[...])
    o_ref[...] = _layer_norm(acc, g_ref[...], b_ref[...])


def _out_ln(ys, ws, x, g, b, alpha):
    t, d = x.shape
    tm = TM_DENSE
    n_in = len(ys)
    return pl.pallas_call(
        functools.partial(_out_ln_kernel, n_in=n_in, alpha=alpha),
        grid=(t // tm,),
        in_specs=([pl.BlockSpec((tm, y.shape[1]), lambda i: (i, 0)) for y in ys]
                  + [_const_spec(w.shape) for w in ws]
                  + [pl.BlockSpec((tm, d), lambda i: (i, 0)), _const_spec((1, d)), _const_spec((1, d))]),
        out_specs=pl.BlockSpec((tm, d), lambda i: (i, 0)),
        out_shape=jax.ShapeDtypeStruct((t, d), F32),
        compiler_params=_cparams("parallel"),
        name="out_proj_ln",
    )(*ys, *[w.astype(BF16) for w in ws], x, g.reshape(1, d), b.reshape(1, d))


def _chunk_tri(tt, inclusive_lower):
    i = np.arange(tt)[:, None]
    j = np.arange(tt)[None, :]
    same = (i // CHUNK) == (j // CHUNK)
    m = same & (j <= i) if inclusive_lower else same
    return jnp.asarray(np.tile(m.astype(np.float32), (1, 3)), BF16)


def _expand_mat(n_heads, width, parts=3):
    e = np.zeros((LANES, n_heads * width), np.float32)
    for h in range(n_heads):
        e[h, h * width:(h + 1) * width] = 1.0
    return jnp.asarray(np.tile(e, (parts, 1)), BF16)


def _segment_ones(n, seg, parts=2):
    i = np.arange(n)
    m = (i[:, None] // seg == i[None, :] // seg).astype(np.float32)
    return jnp.asarray(np.tile(m, (parts, 1)), BF16)


def _wide_masks():
    i = np.arange(CHUNK)[:, None]
    j = (np.arange(GROUP) % CHUNK)[None, :]
    ident = i == j
    causal = i >= j
    strict = i > j
    d8 = (i // 8) == (j // 8)
    o16 = ((i // 16) == (j // 16)) & ((i // 8) == (j // 8) + 1)
    o32 = ((i // 32) == (j // 32)) & ((i // 16) == (j // 16) + 1)
    o64 = (i // 32) == (j // 32) + 1
    return jnp.asarray(np.stack([ident, causal, strict, d8, o16, o32, o64]).astype(np.float32))


M_IDENT, M_CAUSAL, M_STRICT, M_D8, M_O16, M_O32, M_O64 = range(7)


def _bd_mask(n_cols, col_block):
    r = np.arange(GROUP)[:, None] // CHUNK
    c = (np.arange(n_cols)[None, :] % (4 * col_block)) // col_block
    return jnp.asarray((r == c).astype(np.float32))


def _head_masks(n_heads, width):
    m = np.zeros((SUBLANES, n_heads * width), np.float32)
    for h in range(n_heads):
        m[h, h * width:(h + 1) * width] = 1.0
    return jnp.asarray(m)


def _pad_lanes(v):
    v = v.reshape(1, -1).astype(F32)
    return jnp.pad(v, ((0, 0), (0, LANES - v.shape[1])))


def _conv_silu(xbuf, cw_ref, cb_ref, dst, tt, width):
    for start in range(0, width, 4 * LANES):
        sl = slice(start, min(start + 4 * LANES, width))
        acc = cw_ref[0:1, sl] * xbuf[5:5 + tt, sl]
        for k in range(1, CONV_K):
            acc = acc + cw_ref[k:k + 1, sl] * xbuf[5 + k:5 + k + tt, sl]
        if cb_ref is not None:
            acc = acc + cb_ref[:, sl]
        dst[:, sl] = _silu(acc)


def _ssd_kernel(z_ref, xs_ref, bc_ref, dt_ref, cw_ref, cb_ref, alog_ref, dtb_ref, dskip_ref, nw_ref,
                tri_ref, e_ref, wm_ref, mbd_ref, o_ref, xbuf, xc, acol, xdt, hst):
    tt, ncx = xs_ref.shape
    ncb = bc_ref.shape[1]
    gw = ncx // SSD_GROUPS

    @pl.when(pl.program_id(1) == 0)
    def _():
        xbuf[0:SUBLANES, :] = jnp.zeros((SUBLANES, ncx + ncb), F32)
        hst[...] = jnp.zeros_like(hst)

    xbuf[SUBLANES:SUBLANES + tt, 0:ncx] = xs_ref[...]
    xbuf[SUBLANES:SUBLANES + tt, ncx:] = bc_ref[...]
    _conv_silu(xbuf, cw_ref, cb_ref, xc, tt, ncx + ncb)
    xbuf[0:SUBLANES, :] = xbuf[tt:tt + SUBLANES, :]

    dt = _softplus(dt_ref[...] + dtb_ref[...])
    a_cum = _sel_rows(tri_ref, dt * (-jnp.exp(alog_ref[...])))
    for start in range(0, ncx, 4 * LANES):
        sl = slice(start, start + 4 * LANES)
        acol[:, sl] = _sel_cols(a_cum, e_ref.at[:, sl])
        xdt[:, sl] = xc[:, sl] * _sel_cols(dt, e_ref.at[:, sl])

    ident = wm_ref[M_IDENT]
    causal = wm_ref[M_CAUSAL] > 0.5
    mbd = mbd_ref[...]

    def chunk(c, carry):
        r0 = pl.multiple_of(c * CHUNK, CHUNK)
        rows = pl.ds(r0, CHUNK)
        for g in range(SSD_GROUPS):
            gl = slice(g * gw, (g + 1) * gw)
            b_g = xc[rows, ncx + g * SSD_STATE:ncx + (g + 1) * SSD_STATE].astype(BF16)
            c_g = xc[rows, ncx + ncb // 2 + g * SSD_STATE:ncx + ncb // 2 + (g + 1) * SSD_STATE].astype(BF16)
            cb_wide = _dot_nt(c_g, _tile4(b_g))
            a_g = acol[rows, gl]
            a_last = acol[pl.ds(r0 + CHUNK - 1, 1), gl]
            x_g = xdt[rows, gl]
            ys = []
            for hh in range(gw // GROUP):
                hl = slice(hh * GROUP, (hh + 1) * GROUP)
                a_c = a_g[:, hl]
                a_row = jnp.sum(a_c * ident, axis=0, keepdims=True)
                seg = jnp.exp(jnp.where(causal, a_c - a_row, NEG))
                ys.append(_dot((cb_wide * seg).astype(BF16), _block_diag(x_g[:, hl], mbd)))
            y = jnp.concatenate(ys, axis=1)
            h_prev = hst[g]
            y = y + _dot(c_g, h_prev.astype(BF16)) * jnp.exp(a_g)
            hst[g] = h_prev * jnp.exp(a_last) + _dot_tn(b_g, (x_g * jnp.exp(a_last - a_g)).astype(BF16))
            y = y + dskip_ref[:, gl] * xc[rows, gl]
            y = y * _silu(z_ref[rows, gl])
            y = y * lax.rsqrt(jnp.mean(y * y, -1, keepdims=True) + RMS_EPS) * nw_ref[:, gl]
            o_ref[rows, gl] = y
        return carry

    lax.fori_loop(0, tt // CHUNK, chunk, 0)


def _ssd_mixer(p, bsz, s, conv_w, conv_b, a_log, dt_bias, d_skip, norm_w):
    tt = TT_MIX
    nt = s // tt
    n_heads = a_log.shape[0]
    ncx = n_heads * SSD_HEADDIM
    ncb = 2 * SSD_GROUPS * SSD_STATE
    row = lambda b, i: b * nt + i
    consts = [conv_w.astype(F32), conv_b.reshape(1, -1).astype(F32), _pad_lanes(a_log), _pad_lanes(dt_bias),
              jnp.repeat(d_skip.astype(F32), SSD_HEADDIM).reshape(1, ncx), norm_w.reshape(1, ncx).astype(F32),
              _chunk_tri(tt, True), _expand_mat(n_heads, SSD_HEADDIM), _wide_masks(), _bd_mask(GROUP, CHUNK)]
    return pl.pallas_call(
        _ssd_kernel,
        grid=(bsz, nt),
        in_specs=[pl.BlockSpec((tt, ncx), lambda b, i: (row(b, i), 0)),
                  pl.BlockSpec((tt, ncx), lambda b, i: (row(b, i), 1)),
                  pl.BlockSpec((tt, ncb), lambda b, i: (row(b, i), 2 * ncx // ncb)),
                  pl.BlockSpec((tt, LANES), lambda b, i: (row(b, i), (2 * ncx + ncb) // LANES))]
                 + [_const_spec(c.shape, 2) for c in consts],
        out_specs=pl.BlockSpec((tt, ncx), lambda b, i: (row(b, i), 0)),
        out_shape=jax.ShapeDtypeStruct((bsz * s, ncx), F32),
        scratch_shapes=[pltpu.VMEM((tt + SUBLANES, ncx + ncb), F32), pltpu.VMEM((tt, ncx + ncb), F32),
                        pltpu.VMEM((tt, ncx), F32), pltpu.VMEM((tt, ncx), F32),
                        pltpu.VMEM((SSD_GROUPS, SSD_STATE, ncx // SSD_GROUPS), F32)],
        compiler_params=_cparams("parallel", "arbitrary"),
        name="ssd_mixer",
    )(p, p, p, p, *consts)


def _per_head(x, hm_in, hm_outs, reduce_fn, fill):
    outs = [0.0] * len(hm_outs)
    for h in range(MLSTM_HEADS):
        r = reduce_fn(jnp.where(hm_in[h:h + 1, :] > 0.5, x, fill), axis=-1, keepdims=True)
        outs = [o + r * hm[h:h + 1, :] for o, hm in zip(outs, hm_outs)]
    return outs


def _mlstm_kernel(qk_ref, v_ref, og_ref, mi_ref, mf_ref, ib_ref, fb_ref,
                  tri_ref, blk_ref, e2_ref, e5_ref, wm_ref, mbd_ref, mbdv_ref, hm2_ref, hm5_ref,
                  o_ref, b2, li2, bt2, b5, li5, bt5, cst, nst, m5s, m2s):
    tt = qk_ref.shape[0]
    kw = MLSTM_HEADS * MLSTM_DK
    scale = MLSTM_DK ** -0.5

    @pl.when(pl.program_id(1) == 0)
    def _():
        cst[...] = jnp.zeros_like(cst)
        nst[...] = jnp.zeros_like(nst)
        m5s[...] = jnp.zeros_like(m5s)
        m2s[...] = jnp.zeros_like(m2s)

    log_i = mi_ref[...] + ib_ref[...]
    log_f = -_softplus(-(mf_ref[...] + fb_ref[...]))
    b = _sel_rows(tri_ref, log_f)
    bt = _sel_rows(blk_ref, log_f)
    for src, d2, d5 in ((b, b2, b5), (log_i, li2, li5), (bt, bt2, bt5)):
        d2[...] = _sel_cols(src, e2_ref)
        d5[...] = _sel_cols(src, e5_ref)

    ident = wm_ref[M_IDENT]
    causal = wm_ref[M_CAUSAL] > 0.5
    mbd = mbd_ref[...]
    mbdv = mbdv_ref[...]
    hm2 = hm2_ref[...]
    hm5 = hm5_ref[...]

    def chunk(c, carry):
        r0 = pl.multiple_of(c * CHUNK, CHUNK)
        rows = pl.ds(r0, CHUNK)
        q = qk_ref[rows, 0:kw]
        k = qk_ref[rows, kw:2 * kw] * scale
        v = v_ref[rows, :]
        qb = q.astype(BF16)
        bc2, lic2, btc2 = b2[rows, :], li2[rows, :], bt2[rows, :]
        bc5, lic5, btc5 = b5[rows, :], li5[rows, :], bt5[rows, :]
        row_v = jnp.sum((lic2 - bc2) * ident, axis=0, keepdims=True)
        d_log = jnp.where(causal, bc2 + row_v, NEG)
        mi2, mi5 = _per_head(d_log, hm2, (hm2, hm5), jnp.max, NEG)
        s_qk = _dot_nt(qb, _block_diag(k, mbd)) * jnp.exp(d_log - mi2)
        num_i = _dot(s_qk.astype(BF16), _block_diag(v, mbdv))
        (den_i,) = _per_head(s_qk, hm2, (hm5,), jnp.sum, 0.0)
        w2 = btc2 - bc2 + lic2
        w5 = btc5 - bc5 + lic5
        me2 = jnp.max(w2, axis=0, keepdims=True)
        me5 = jnp.max(w5, axis=0, keepdims=True)
        k_end = k * jnp.exp(w2 - me2)
        c_end = _dot_tn(k_end.astype(BF16), v.astype(BF16)) * mbdv
        n_end = jnp.sum(k_end, axis=0, keepdims=True)
        m5, m2 = m5s[...], m2s[...]
        c_st, n_st = cst[...], nst[...]
        a5 = bc5 + m5
        m_t = jnp.maximum(a5, mi5)
        ea, ei = jnp.exp(a5 - m_t), jnp.exp(mi5 - m_t)
        (qn,) = _per_head(q * n_st, hm2, (hm5,), jnp.sum, 0.0)
        num = ea * _dot(qb, c_st.astype(BF16)) + ei * num_i
        den = ea * qn + ei * den_i
        h = num / jnp.maximum(jnp.abs(den), jnp.exp(-m_t))
        o_ref[rows, :] = _sigmoid(og_ref[rows, :]) * h
        bl5, bl2 = btc5[0:1, :], btc2[0:1, :]
        mn5 = jnp.maximum(bl5 + m5, me5)
        mn2 = jnp.maximum(bl2 + m2, me2)
        cst[...] = c_st * jnp.exp(bl5 + m5 - mn5) + c_end * jnp.exp(me5 - mn5)
        nst[...] = n_st * jnp.exp(bl2 + m2 - mn2) + n_end * jnp.exp(me2 - mn2)
        m5s[...] = mn5
        m2s[...] = mn2
        return carry

    lax.fori_loop(0, tt // CHUNK, chunk, 0)


def _mlstm_mixer(p, bsz, s, col0, i_bias, f_bias):
    tt = TT_MIX
    nt = s // tt
    kw = MLSTM_HEADS * MLSTM_DK
    vw = MLSTM_HEADS * MLSTM_DV
    row = lambda b, i: b * nt + i
    c512 = col0 // vw
    gate0 = (col0 + 3 * vw) // LANES + 2
    consts = [_pad_lanes(i_bias), _pad_lanes(f_bias), _chunk_tri(tt, True), _chunk_tri(tt, False),
              _expand_mat(MLSTM_HEADS, MLSTM_DK), _expand_mat(MLSTM_HEADS, MLSTM_DV), _wide_masks(),
              _bd_mask(kw, MLSTM_DK), _bd_mask(vw, MLSTM_DV),
              _head_masks(MLSTM_HEADS, MLSTM_DK), _head_masks(MLSTM_HEADS, MLSTM_DV)]
    return pl.pallas_call(
        _mlstm_kernel,
        grid=(bsz, nt),
        in_specs=[pl.BlockSpec((tt, 2 * kw), lambda b, i: (row(b, i), c512)),
                  pl.BlockSpec((tt, vw), lambda b, i: (row(b, i), c512 + 1)),
                  pl.BlockSpec((tt, vw), lambda b, i: (row(b, i), c512 + 2)),
                  pl.BlockSpec((tt, LANES), lambda b, i: (row(b, i), gate0)),
                  pl.BlockSpec((tt, LANES), lambda b, i: (row(b, i), gate0 + 1))]
                 + [_const_spec(c.shape, 2) for c in consts],
        out_specs=pl.BlockSpec((tt, vw), lambda b, i: (row(b, i), 0)),
        out_shape=jax.ShapeDtypeStruct((bsz * s, vw), F32),
        scratch_shapes=[pltpu.VMEM((tt, kw), F32)] * 3 + [pltpu.VMEM((tt, vw), F32)] * 3
                       + [pltpu.VMEM((kw, vw), F32), pltpu.VMEM((1, kw), F32),
                          pltpu.VMEM((1, vw), F32), pltpu.VMEM((1, kw), F32)],
        compiler_params=_cparams("parallel", "arbitrary"),
        name="mlstm_mixer",
    )(p, p, p, p, p, *consts)


def _unit_lower_inverse(l_wide, wm_ref, mbd):
    def mm(a, b):
        return _dot(a.astype(BF16), _block_diag(b, mbd))

    ident = wm_ref[M_IDENT]
    l8 = l_wide * wm_ref[M_D8]
    l8_2 = mm(l8, l8)
    l8_4 = mm(l8_2, l8_2)
    t = mm(mm(ident - l8, ident + l8_2), ident + l8_4)
    for level in (M_O16, M_O32, M_O64):
        t = t - mm(mm(t, l_wide * wm_ref[level]), t)
    return t


def _gdn_kernel(qkv_ref, z_ref, gb_ref, ga_ref, cw_ref, alog_ref, dtb_ref, nw_ref,
                tri_ref, e_ref, seg_ref, wm_ref, mbd_ref, mbd2_ref,
                o_ref, xbuf, xc, gcw, btw, u0b, wdb, atb, qdb, kdb, ob, sst):
    tt = z_ref.shape[0]
    hw = GDN_HEADS * GDN_DK
    n_groups = hw // GROUP

    @pl.when(pl.program_id(1) == 0)
    def _():
        xbuf[0:SUBLANES, :] = jnp.zeros((SUBLANES, 3 * hw), F32)
        sst[...] = jnp.zeros_like(sst)

    xbuf[SUBLANES:SUBLANES + tt, :] = qkv_ref[...]
    _conv_silu(xbuf, cw_ref, None, xc, tt, 3 * hw)
    xbuf[0:SUBLANES, :] = xbuf[tt:tt + SUBLANES, :]

    for j, s in ((0, GDN_DK ** -0.5), (1, 1.0)):
        sl = slice(j * hw, (j + 1) * hw)
        x = xc[:, sl]
        xc[:, sl] = x * (lax.rsqrt(_sel_cols(x * x, seg_ref, 2) + L2_EPS) * s)

    beta = _sigmoid(gb_ref[...])
    g = -jnp.exp(alog_ref[...]) * _softplus(ga_ref[...] + dtb_ref[...])
    gcw[...] = _sel_cols(_sel_rows(tri_ref, g), e_ref)
    btw[...] = _sel_cols(beta, e_ref)

    ident = wm_ref[M_IDENT]
    causal = wm_ref[M_CAUSAL] > 0.5
    strict = wm_ref[M_STRICT]
    mbd = mbd_ref[...]
    mbd2 = mbd2_ref[...]

    for c in range(tt // CHUNK):
        rows = slice(c * CHUNK, (c + 1) * CHUNK)
        for grp in range(n_groups):
            l = slice(grp * GROUP, (grp + 1) * GROUP)
            gc, bt = gcw[rows, l], btw[rows, l]
            q = xc[rows, grp * GROUP:(grp + 1) * GROUP]
            k = xc[rows, hw + grp * GROUP:hw + (grp + 1) * GROUP]
            v = xc[rows, 2 * hw + grp * GROUP:2 * hw + (grp + 1) * GROUP]
            g_row = jnp.sum(gc * ident, axis=0, keepdims=True)
            dec = jnp.exp(jnp.where(causal, gc - g_row, NEG))
            eg = jnp.exp(gc)
            g_last = gc[CHUNK - 1:CHUNK, :]
            kb = k * bt
            kbd = _block_diag(k, mbd)
            lower = _dot_nt(kb.astype(BF16), kbd) * dec * strict
            atb[rows, l] = _dot_nt(q.astype(BF16), kbd) * dec
            t_inv = _unit_lower_inverse(lower, wm_ref, mbd)
            rhs = jnp.concatenate([v * bt, kb * eg], axis=1)
            sol = _dot(t_inv.astype(BF16), _block_diag(rhs, mbd2))
            u0b[rows, l] = sol[:, :GROUP]
            wdb[rows, l] = sol[:, GROUP:]
            qdb[rows, l] = q * eg
            kdb[rows, l] = k * jnp.exp(g_last - gc)

    def chunk(c, carry):
        r0 = pl.multiple_of(c * CHUNK, CHUNK)
        rows = pl.ds(r0, CHUNK)
        for grp in range(n_groups):
            l = slice(grp * GROUP, (grp + 1) * GROUP)
            st = sst[grp]
            wq = jnp.concatenate([wdb[rows, l], qdb[rows, l]], axis=0).astype(BF16)
            r = _dot(wq, st.astype(BF16))
            u = u0b[rows, l] - r[:CHUNK]
            ob[rows, l] = r[CHUNK:] + _dot(atb[rows, l].astype(BF16), _block_diag(u, mbd))
            g_last = gcw[pl.ds(r0 + CHUNK - 1, 1), l]
            sst[grp] = st * jnp.exp(g_last) + _dot_tn(kdb[rows, l].astype(BF16), u.astype(BF16)) * mbd
        return carry

    lax.fori_loop(0, tt // CHUNK, chunk, 0)

    o = ob[...]
    ms = _sel_cols(o * o, seg_ref, 2) * (1.0 / GDN_DV)
    o_ref[...] = o * lax.rsqrt(ms + RMS_EPS) * nw_ref[...] * _silu(z_ref[...])


def _gdn_mixer(p, bsz, s, conv_w, a_log, dt_bias, norm_w):
    tt = TT_MIX
    nt = s // tt
    hw = GDN_HEADS * GDN_DK
    row = lambda b, i: b * nt + i
    gate0 = (p.shape[1] - 4 * LANES) // LANES
    consts = [conv_w.astype(F32), _pad_lanes(a_log), _pad_lanes(dt_bias),
              jnp.tile(norm_w.astype(F32), GDN_HEADS).reshape(1, hw),
              _chunk_tri(tt, True), _expand_mat(GDN_HEADS, GDN_DK), _segment_ones(hw, GDN_DK), _wide_masks(),
              _bd_mask(GROUP, CHUNK), _bd_mask(2 * GROUP, CHUNK)]
    return pl.pallas_call(
        _gdn_kernel,
        grid=(bsz, nt),
        in_specs=[pl.BlockSpec((tt, 3 * hw), lambda b, i: (row(b, i), 0)),
                  pl.BlockSpec((tt, hw), lambda b, i: (row(b, i), 3)),
                  pl.BlockSpec((tt, LANES), lambda b, i: (row(b, i), gate0)),
                  pl.BlockSpec((tt, LANES), lambda b, i: (row(b, i), gate0 + 1))]
                 + [_const_spec(c.shape, 2) for c in consts],
        out_specs=pl.BlockSpec((tt, hw), lambda b, i: (row(b, i), 0)),
        out_shape=jax.ShapeDtypeStruct((bsz * s, hw), F32),
        scratch_shapes=[pltpu.VMEM((tt + SUBLANES, 3 * hw), F32), pltpu.VMEM((tt, 3 * hw), F32)]
                       + [pltpu.VMEM((tt, hw), F32)] * 8
                       + [pltpu.VMEM((hw // GROUP, GROUP, GROUP), F32)],
        compiler_params=_cparams("parallel", "arbitrary"),
        name="gdn_mixer",
    )(p, p, p, p, *consts)


def _pad_cols(w):
    return jnp.pad(w, ((0, 0), (0, LANES - w.shape[1])))


def _hybrid_w_in(w):
    hw = GDN_HEADS * GDN_DK
    a_main = 4 * hw
    b0 = a_main + 2 * GDN_HEADS
    b_main = 2 * MLSTM_HEADS * MLSTM_DK + 2 * MLSTM_HEADS * MLSTM_DV
    return jnp.concatenate([w[:, :a_main], w[:, b0:b0 + b_main],
                            _pad_cols(w[:, a_main:a_main + GDN_HEADS]), _pad_cols(w[:, a_main + GDN_HEADS:b0]),
                            _pad_cols(w[:, b0 + b_main:b0 + b_main + MLSTM_HEADS]),
                            _pad_cols(w[:, b0 + b_main + MLSTM_HEADS:])], axis=1)


def _ssd_w_in(w, n_heads):
    main = w.shape[1] - n_heads
    return jnp.concatenate([w[:, :main], _pad_cols(w[:, main:])], axis=1)


def kernel(x, ffn_pre_w_gate, ffn_pre_w_up, ffn_pre_w_down, ln_pre_g, ln_pre_b, hyb_w_in, hyb_conv_w, gdn_a_log, gdn_dt_bias, gdn_norm_w, mlstm_i_bias, mlstm_f_bias, hyb_w_out, ssd_w_in, ssd_conv_w, ssd_conv_b, ssd_a_log, ssd_dt_bias, ssd_d_skip, ssd_norm_w, ssd_w_out, ln_mix_g, ln_mix_b, ffn_post_w_gate, ffn_post_w_up, ffn_post_w_down, ln_post_g, ln_post_b):
    bsz, s, d = x.shape
    depth = ffn_pre_w_gate.shape[0]
    alpha = (2 * depth) ** 0.25
    gdn_w = GDN_HEADS * GDN_DV
    h = x.reshape(bsz * s, d)
    for l in range(depth):
        h = _ffn(h, ffn_pre_w_gate[l], ffn_pre_w_up[l], ffn_pre_w_down[l], ln_pre_g[l], ln_pre_b[l], alpha)
        j = l // 2
        if l % 2 == 0:
            p = _proj(h, _hybrid_w_in(hyb_w_in[j]))
            o_a = _gdn_mixer(p, bsz, s, hyb_conv_w[j], gdn_a_log[j], gdn_dt_bias[j], gdn_norm_w[j])
            o_b = _mlstm_mixer(p, bsz, s, 4 * gdn_w, mlstm_i_bias[j], mlstm_f_bias[j])
            h = _out_ln([o_a, o_b], [hyb_w_out[j][:gdn_w], hyb_w_out[j][gdn_w:]], h,
                        ln_mix_g[l], ln_mix_b[l], alpha)
        else:
            p = _proj(h, _ssd_w_in(ssd_w_in[j], ssd_a_log.shape[1]))
            y = _ssd_mixer(p, bsz, s, ssd_conv_w[j], ssd_conv_b[j], ssd_a_log[j], ssd_dt_bias[j],
                           ssd_d_skip[j], ssd_norm_w[j])
            h = _out_ln([y], [ssd_w_out[j]], h, ln_mix_g[l], ln_mix_b[l], alpha)
        h = _ffn(h, ffn_post_w_gate[l], ffn_post_w_up[l], ffn_post_w_down[l], ln_post_g[l], ln_post_b[l], alpha)
    return h.reshape(bsz, s, d)
```

```python
import functools

import jax
import jax.numpy as jnp
import numpy as np
from jax import lax
from jax.experimental import pallas as pl
from jax.experimental.pallas import tpu as pltpu

F32 = jnp.float32
BF16 = jnp.bfloat16

LN_EPS = 1e-5
RMS_EPS = 1e-6
L2_EPS = 1e-6
CHUNK = 64
CONV_K = 4
NEG = -1e30

GDN_HEADS, GDN_DK, GDN_DV = 8, 64, 64
MLSTM_HEADS, MLSTM_DK, MLSTM_DV = 4, 64, 128
SSD_HEADDIM, SSD_GROUPS, SSD_STATE = 64, 4, 128

LANES = 128
SUBLANES = 8
VMEM_LIMIT = 56 * 1024 * 1024
GROUP = 4 * CHUNK

TM_DENSE = 512
TT_MIX = 256


def _cparams(*sem):
    return pltpu.CompilerParams(dimension_semantics=sem, vmem_limit_bytes=VMEM_LIMIT)


def _dot(a, b):
    return jnp.dot(a, b, preferred_element_type=F32)


def _dot_nt(a, b):
    return lax.dot_general(a, b, (((1,), (1,)), ((), ())), preferred_element_type=F32)


def _dot_tn(a, b):
    return lax.dot_general(a, b, (((0,), (0,)), ((), ())), preferred_element_type=F32)


def _layer_norm(y, g, b):
    mu = jnp.mean(y, -1, keepdims=True)
    yc = y - mu
    var = jnp.mean(yc * yc, -1, keepdims=True)
    return yc * lax.rsqrt(var + LN_EPS) * g + b


def _sigmoid(x):
    return 1.0 / (1.0 + jnp.exp(-x))


def _silu(x):
    return x * _sigmoid(x)


def _softplus(x):
    return jnp.maximum(x, 0.0) + jnp.log(1.0 + jnp.exp(-jnp.abs(x)))


def _split(x, parts):
    out = []
    r = x
    for p in range(parts):
        h = r.astype(BF16)
        out.append(h)
        if p + 1 < parts:
            r = r - h.astype(F32)
    return out


def _sel_rows(m01x3_ref, x):
    return _dot(m01x3_ref[...], jnp.concatenate(_split(x, 3), axis=0))


def _sel_cols(x, m01x3_ref, parts=3):
    return _dot(jnp.concatenate(_split(x, parts), axis=1), m01x3_ref[...])


def _tile4(x):
    return jnp.concatenate([x, x, x, x], axis=0)


def _block_diag(x, mask_bf16):
    return _tile4(x.astype(BF16)) * mask_bf16


def _ffn_kernel(x_ref, wg_ref, wu_ref, wd_ref, g_ref, b_ref, o_ref, *, fc, alpha):
    x = x_ref[...]
    xb = x.astype(BF16)
    acc = jnp.zeros_like(x)
    for c in range(wg_ref.shape[1] // fc):
        sl = slice(c * fc, (c + 1) * fc)
        hg = _dot(xb, wg_ref[:, sl])
        hu = _dot(xb, wu_ref[:, sl])
        acc = acc + _dot((_silu(hg) * hu).astype(BF16), wd_ref[sl, :])
    o_ref[...] = _layer_norm(alpha * x + 0.5 * acc, g_ref[...], b_ref[...])


def _const_spec(shape, grid_rank=1):
    zeros = (0,) * len(shape)
    if grid_rank == 1:
        return pl.BlockSpec(shape, lambda i: zeros, pipeline_mode=pl.Buffered(1))
    return pl.BlockSpec(shape, lambda b, i: zeros, pipeline_mode=pl.Buffered(1))


def _ffn(x, wg, wu, wd, g, b, alpha):
    t, d = x.shape
    f = wg.shape[1]
    tm = TM_DENSE
    return pl.pallas_call(
        functools.partial(_ffn_kernel, fc=2 * LANES, alpha=alpha),
        grid=(t // tm,),
        in_specs=[pl.BlockSpec((tm, d), lambda i: (i, 0)),
                  _const_spec((d, f)), _const_spec((d, f)), _const_spec((f, d)),
                  _const_spec((1, d)), _const_spec((1, d))],
        out_specs=pl.BlockSpec((tm, d), lambda i: (i, 0)),
        out_shape=jax.ShapeDtypeStruct((t, d), F32),
        compiler_params=_cparams("parallel"),
        name="ffn",
    )(x, wg.astype(BF16), wu.astype(BF16), wd.astype(BF16), g.reshape(1, d), b.reshape(1, d))


def _proj_kernel(x_ref, w_ref, o_ref, *, tn):
    xb = x_ref[...].astype(BF16)
    n = w_ref.shape[1]
    for start in range(0, n, tn):
        sl = slice(start, min(start + tn, n))
        o_ref[:, sl] = _dot(xb, w_ref[:, sl])


def _proj(x, w):
    t, d = x.shape
    n = w.shape[1]
    tm = TM_DENSE
    return pl.pallas_call(
        functools.partial(_proj_kernel, tn=4 * LANES),
        grid=(t // tm,),
        in_specs=[pl.BlockSpec((tm, d), lambda i: (i, 0)), _const_spec((d, n))],
        out_specs=pl.BlockSpec((tm, n), lambda i: (i, 0)),
        out_shape=jax.ShapeDtypeStruct((t, n), F32),
        compiler_params=_cparams("parallel"),
        name="in_proj",
    )(x, w.astype(BF16))


def _out_ln_kernel(*refs, n_in, alpha):
    y_refs, w_refs = refs[:n_in], refs[n_in:2 * n_in]
    x_ref, g_ref, b_ref, o_ref = refs[2 * n_in:]
    acc = alpha * x_ref[...]
    for y_ref, w_ref in zip(y_refs, w_refs):
        acc = acc + _dot(y_ref[...].astype(BF16), w_ref[...])
    o_ref[...] = _layer_norm(acc, g_ref[...], b_ref[...])


def _out_ln(ys, ws, x, g, b, alpha):
    t, d = x.shape
    tm = TM_DENSE
    n_in = len(ys)
    return pl.pallas_call(
        functools.partial(_out_ln_kernel, n_in=n_in, alpha=alpha),
        grid=(t // tm,),
        in_specs=([pl.BlockSpec((tm, y.shape[1]), lambda i: (i, 0)) for y in ys]
                  + [_const_spec(w.shape) for w in ws]
                  + [pl.BlockSpec((tm, d), lambda i: (i, 0)), _const_spec((1, d)), _const_spec((1, d))]),
        out_specs=pl.BlockSpec((tm, d), lambda i: (i, 0)),
        out_shape=jax.ShapeDtypeStruct((t, d), F32),
        compiler_params=_cparams("parallel"),
        name="out_proj_ln",
    )(*ys, *[w.astype(BF16) for w in ws], x, g.reshape(1, d), b.reshape(1, d))


def _chunk_tri(tt, inclusive_lower):
    i = np.arange(tt)[:, None]
    j = np.arange(tt)[None, :]
    same = (i // CHUNK) == (j // CHUNK)
    m = same & (j <= i) if inclusive_lower else same
    return jnp.asarray(np.tile(m.astype(np.float32), (1, 3)), BF16)


def _expand_mat(n_heads, width, parts=3):
    e = np.zeros((LANES, n_heads * width), np.float32)
    for h in range(n_heads):
        e[h, h * width:(h + 1) * width] = 1.0
    return jnp.asarray(np.tile(e, (parts, 1)), BF16)


def _segment_ones(n, seg, parts=2):
    i = np.arange(n)
    m = (i[:, None] // seg == i[None, :] // seg).astype(np.float32)
    return jnp.asarray(np.tile(m, (parts, 1)), BF16)


def _wide_masks():
    i = np.arange(CHUNK)[:, None]
    j = (np.arange(GROUP) % CHUNK)[None, :]
    ident = i == j
    causal = i >= j
    strict = i > j
    d8 = (i // 8) == (j // 8)
    o16 = ((i // 16) == (j // 16)) & ((i // 8) == (j // 8) + 1)
    o32 = ((i // 32) == (j // 32)) & ((i // 16) == (j // 16) + 1)
    o64 = (i // 32) == (j // 32) + 1
    return jnp.asarray(np.stack([ident, causal, strict, d8, o16, o32, o64]).astype(np.float32))


M_IDENT, M_CAUSAL, M_STRICT, M_D8, M_O16, M_O32, M_O64 = range(7)


def _bd_mask(n_cols, col_block):
    r = np.arange(GROUP)[:, None] // CHUNK
    c = (np.arange(n_cols)[None, :] % (4 * col_block)) // col_block
    return jnp.asarray((r == c).astype(np.float32))


def _head_masks(n_heads, width):
    m = np.zeros((SUBLANES, n_heads * width), np.float32)
    for h in range(n_heads):
        m[h, h * width:(h + 1) * width] = 1.0
    return jnp.asarray(m)


def _pad_lanes(v):
    v = v.reshape(1, -1).astype(F32)
    return jnp.pad(v, ((0, 0), (0, LANES - v.shape[1])))


def _conv_silu(xbuf, cw_ref, cb_ref, dst, tt, width):
    for start in range(0, width, 4 * LANES):
        sl = slice(start, min(start + 4 * LANES, width))
        acc = cw_ref[0:1, sl] * xbuf[5:5 + tt, sl]
        for k in range(1, CONV_K):
            acc = acc + cw_ref[k:k + 1, sl] * xbuf[5 + k:5 + k + tt, sl]
        if cb_ref is not None:
            acc = acc + cb_ref[:, sl]
        dst[:, sl] = _silu(acc)


def _ssd_kernel(z_ref, xs_ref, bc_ref, dt_ref, cw_ref, cb_ref, alog_ref, dtb_ref, dskip_ref, nw_ref,
                tri_ref, e_ref, wm_ref, mbd_ref, o_ref, xbuf, xc, acol, xdt, hst):
    tt, ncx = xs_ref.shape
    ncb = bc_ref.shape[1]
    gw = ncx // SSD_GROUPS

    @pl.when(pl.program_id(1) == 0)
    def _():
        xbuf[0:SUBLANES, :] = jnp.zeros((SUBLANES, ncx + ncb), F32)
        hst[...] = jnp.zeros_like(hst)

    xbuf[SUBLANES:SUBLANES + tt, 0:ncx] = xs_ref[...]
    xbuf[SUBLANES:SUBLANES + tt, ncx:] = bc_ref[...]
    _conv_silu(xbuf, cw_ref, cb_ref, xc, tt, ncx + ncb)
    xbuf[0:SUBLANES, :] = xbuf[tt:tt + SUBLANES, :]

    dt = _softplus(dt_ref[...] + dtb_ref[...])
    a_cum = _sel_rows(tri_ref, dt * (-jnp.exp(alog_ref[...])))
    for start in range(0, ncx, 4 * LANES):
        sl = slice(start, start + 4 * LANES)
        acol[:, sl] = _sel_cols(a_cum, e_ref.at[:, sl])
        xdt[:, sl] = xc[:, sl] * _sel_cols(dt, e_ref.at[:, sl])

    ident = wm_ref[M_IDENT]
    causal = wm_ref[M_CAUSAL] > 0.5
    mbd = mbd_ref[...].astype(BF16)
    groups = range(SSD_GROUPS)
    halves = range(gw // GROUP)

    def chunk(c, carry):
        r0 = pl.multiple_of(c * CHUNK, CHUNK)
        rows = pl.ds(r0, CHUNK)
        gls = [slice(g * gw, (g + 1) * gw) for g in groups]
        b_gs = [xc[rows, ncx + g * SSD_STATE:ncx + (g + 1) * SSD_STATE].astype(BF16) for g in groups]
        c_gs = [xc[rows, ncx + ncb // 2 + g * SSD_STATE:ncx + ncb // 2 + (g + 1) * SSD_STATE].astype(BF16)
                for g in groups]
        cb_wide = [_dot_nt(c_gs[g], _tile4(b_gs[g])) for g in groups]
        a_gs = [acol[rows, gl] for gl in gls]
        a_lasts = [acol[pl.ds(r0 + CHUNK - 1, 1), gl] for gl in gls]
        x_gs = [xdt[rows, gl] for gl in gls]
        lhs, rhs = [], []
        for g in groups:
            for hh in halves:
                hl = slice(hh * GROUP, (hh + 1) * GROUP)
                a_c = a_gs[g][:, hl]
                a_row = jnp.sum(a_c * ident, axis=0, keepdims=True)
                seg = jnp.exp(jnp.where(causal, a_c - a_row, NEG))
                lhs.append((cb_wide[g] * seg).astype(BF16))
                rhs.append(_block_diag(x_gs[g][:, hl], mbd))
        y_diag = [_dot(a, b) for a, b in zip(lhs, rhs)]
        h_prev = [hst[g] for g in groups]
        y_off = [_dot(c_gs[g], h_prev[g].astype(BF16)) for g in groups]
        upd = [_dot_tn(b_gs[g], (x_gs[g] * jnp.exp(a_lasts[g] - a_gs[g])).astype(BF16)) for g in groups]
        for g in groups:
            gl = gls[g]
            hst[g] = h_prev[g] * jnp.exp(a_lasts[g]) + upd[g]
            y = jnp.concatenate(y_diag[g * len(halves):(g + 1) * len(halves)], axis=1)
            y = y + y_off[g] * jnp.exp(a_gs[g]) + dskip_ref[:, gl] * xc[rows, gl]
            y = y * _silu(z_ref[rows, gl])
            o_ref[rows, gl] = y * lax.rsqrt(jnp.mean(y * y, -1, keepdims=True) + RMS_EPS) * nw_ref[:, gl]
        return carry

    lax.fori_loop(0, tt // CHUNK, chunk, 0)


def _ssd_mixer(p, bsz, s, conv_w, conv_b, a_log, dt_bias, d_skip, norm_w):
    tt = TT_MIX
    nt = s // tt
    n_heads = a_log.shape[0]
    ncx = n_heads * SSD_HEADDIM
    ncb = 2 * SSD_GROUPS * SSD_STATE
    row = lambda b, i: b * nt + i
    consts = [conv_w.astype(F32), conv_b.reshape(1, -1).astype(F32), _pad_lanes(a_log), _pad_lanes(dt_bias),
              jnp.repeat(d_skip.astype(F32), SSD_HEADDIM).reshape(1, ncx), norm_w.reshape(1, ncx).astype(F32),
              _chunk_tri(tt, True), _expand_mat(n_heads, SSD_HEADDIM), _wide_masks(), _bd_mask(GROUP, CHUNK)]
    return pl.pallas_call(
        _ssd_kernel,
        grid=(bsz, nt),
        in_specs=[pl.BlockSpec((tt, ncx), lambda b, i: (row(b, i), 0)),
                  pl.BlockSpec((tt, ncx), lambda b, i: (row(b, i), 1)),
                  pl.BlockSpec((tt, ncb), lambda b, i: (row(b, i), 2 * ncx // ncb)),
                  pl.BlockSpec((tt, LANES), lambda b, i: (row(b, i), (2 * ncx + ncb) // LANES))]
                 + [_const_spec(c.shape, 2) for c in consts],
        out_specs=pl.BlockSpec((tt, ncx), lambda b, i: (row(b, i), 0)),
        out_shape=jax.ShapeDtypeStruct((bsz * s, ncx), F32),
        scratch_shapes=[pltpu.VMEM((tt + SUBLANES, ncx + ncb), F32), pltpu.VMEM((tt, ncx + ncb), F32),
                        pltpu.VMEM((tt, ncx), F32), pltpu.VMEM((tt, ncx), F32),
                        pltpu.VMEM((SSD_GROUPS, SSD_STATE, ncx // SSD_GROUPS), F32)],
        compiler_params=_cparams("parallel", "arbitrary"),
        name="ssd_mixer",
    )(p, p, p, p, *consts)


def _per_head(x, hm_in, hm_outs, reduce_fn, fill):
    outs = [0.0] * len(hm_outs)
    for h in range(MLSTM_HEADS):
        r = reduce_fn(jnp.where(hm_in[h:h + 1, :] > 0.5, x, fill), axis=-1, keepdims=True)
        outs = [o + r * hm[h:h + 1, :] for o, hm in zip(outs, hm_outs)]
    return outs


def _mlstm_kernel(qk_ref, v_ref, og_ref, mi_ref, mf_ref, ib_ref, fb_ref,
                  tri_ref, blk_ref, e2_ref, e5_ref, wm_ref, mbd_ref, mbdv_ref, hm2_ref, hm5_ref,
                  o_ref, b2, li2, bt2, b5, li5, bt5, cst, nst, m5s, m2s):
    tt = qk_ref.shape[0]
    kw = MLSTM_HEADS * MLSTM_DK
    scale = MLSTM_DK ** -0.5

    @pl.when(pl.program_id(1) == 0)
    def _():
        cst[...] = jnp.zeros_like(cst)
        nst[...] = jnp.zeros_like(nst)
        m5s[...] = jnp.zeros_like(m5s)
        m2s[...] = jnp.zeros_like(m2s)

    log_i = mi_ref[...] + ib_ref[...]
    log_f = -_softplus(-(mf_ref[...] + fb_ref[...]))
    b = _sel_rows(tri_ref, log_f)
    bt = _sel_rows(blk_ref, log_f)
    for src, d2, d5 in ((b, b2, b5), (log_i, li2, li5), (bt, bt2, bt5)):
        d2[...] = _sel_cols(src, e2_ref)
        d5[...] = _sel_cols(src, e5_ref)

    ident = wm_ref[M_IDENT]
    causal = wm_ref[M_CAUSAL] > 0.5
    mbd = mbd_ref[...].astype(BF16)
    mbdv_f = mbdv_ref[...]
    mbdv = mbdv_f.astype(BF16)
    hm2 = hm2_ref[...]
    hm5 = hm5_ref[...]
    chunks = [slice(c * CHUNK, (c + 1) * CHUNK) for c in range(tt // CHUNK)]

    qs = [qk_ref[rows, 0:kw] for rows in chunks]
    ks = [qk_ref[rows, kw:2 * kw] * scale for rows in chunks]
    vs = [v_ref[rows, :] for rows in chunks]
    d_logs, mi2s, mi5s, me2s, me5s, k_ends = [], [], [], [], [], []
    for rows, k in zip(chunks, ks):
        bc2, lic2 = b2[rows, :], li2[rows, :]
        row_v = jnp.sum((lic2 - bc2) * ident, axis=0, keepdims=True)
        d_log = jnp.where(causal, bc2 + row_v, NEG)
        mi2, mi5 = _per_head(d_log, hm2, (hm2, hm5), jnp.max, NEG)
        d_logs.append(d_log)
        mi2s.append(mi2)
        mi5s.append(mi5)
        w2 = bt2[rows, :] - bc2 + lic2
        me2 = jnp.max(w2, axis=0, keepdims=True)
        me2s.append(me2)
        me5s.append(jnp.max(bt5[rows, :] - b5[rows, :] + li5[rows, :], axis=0, keepdims=True))
        k_ends.append(k * jnp.exp(w2 - me2))
    s_qks = [_dot_nt(q.astype(BF16), _block_diag(k, mbd)) * jnp.exp(d_log - mi2)
             for q, k, d_log, mi2 in zip(qs, ks, d_logs, mi2s)]
    c_ends = [_dot_tn(k_end.astype(BF16), v.astype(BF16)) * mbdv_f for k_end, v in zip(k_ends, vs)]
    num_is = [_dot(s_qk.astype(BF16), _block_diag(v, mbdv)) for s_qk, v in zip(s_qks, vs)]
    den_is = [_per_head(s_qk, hm2, (hm5,), jnp.sum, 0.0)[0] for s_qk in s_qks]

    c_st, n_st, m5, m2 = cst[...], nst[...], m5s[...], m2s[...]
    c_sts, n_sts, m5l = [], [], []
    for rows, c_end, k_end, me5, me2 in zip(chunks, c_ends, k_ends, me5s, me2s):
        c_sts.append(c_st)
        n_sts.append(n_st)
        m5l.append(m5)
        bl5, bl2 = bt5[rows.start:rows.start + 1, :], bt2[rows.start:rows.start + 1, :]
        mn5 = jnp.maximum(bl5 + m5, me5)
        mn2 = jnp.maximum(bl2 + m2, me2)
        c_st = c_st * jnp.exp(bl5 + m5 - mn5) + c_end * jnp.exp(me5 - mn5)
        n_st = n_st * jnp.exp(bl2 + m2 - mn2) + jnp.sum(k_end, axis=0, keepdims=True) * jnp.exp(me2 - mn2)
        m5, m2 = mn5, mn2
    cst[...], nst[...], m5s[...], m2s[...] = c_st, n_st, m5, m2
    q_cs = [_dot(q.astype(BF16), c.astype(BF16)) for q, c in zip(qs, c_sts)]
    for rows, q, q_c, n, m, mi5, num_i, den_i in zip(chunks, qs, q_cs, n_sts, m5l, mi5s, num_is, den_is):
        a5 = b5[rows, :] + m
        m_t = jnp.maximum(a5, mi5)
        ea, ei = jnp.exp(a5 - m_t), jnp.exp(mi5 - m_t)
        (qn,) = _per_head(q * n, hm2, (hm5,), jnp.sum, 0.0)
        h = (ea * q_c + ei * num_i) / jnp.maximum(jnp.abs(ea * qn + ei * den_i), jnp.exp(-m_t))
        o_ref[rows, :] = _sigmoid(og_ref[rows, :]) * h


def _mlstm_mixer(p, bsz, s, col0, i_bias, f_bias):
    tt = TT_MIX
    nt = s // tt
    kw = MLSTM_HEADS * MLSTM_DK
    vw = MLSTM_HEADS * MLSTM_DV
    row = lambda b, i: b * nt + i
    c512 = col0 // vw
    gate0 = (col0 + 3 * vw) // LANES + 2
    consts = [_pad_lanes(i_bias), _pad_lanes(f_bias), _chunk_tri(tt, True), _chunk_tri(tt, False),
              _expand_mat(MLSTM_HEADS, MLSTM_DK), _expand_mat(MLSTM_HEADS, MLSTM_DV), _wide_masks(),
              _bd_mask(kw, MLSTM_DK), _bd_mask(vw, MLSTM_DV),
              _head_masks(MLSTM_HEADS, MLSTM_DK), _head_masks(MLSTM_HEADS, MLSTM_DV)]
    return pl.pallas_call(
        _mlstm_kernel,
        grid=(bsz, nt),
        in_specs=[pl.BlockSpec((tt, 2 * kw), lambda b, i: (row(b, i), c512)),
                  pl.BlockSpec((tt, vw), lambda b, i: (row(b, i), c512 + 1)),
                  pl.BlockSpec((tt, vw), lambda b, i: (row(b, i), c512 + 2)),
                  pl.BlockSpec((tt, LANES), lambda b, i: (row(b, i), gate0)),
                  pl.BlockSpec((tt, LANES), lambda b, i: (row(b, i), gate0 + 1))]
                 + [_const_spec(c.shape, 2) for c in consts],
        out_specs=pl.BlockSpec((tt, vw), lambda b, i: (row(b, i), 0)),
        out_shape=jax.ShapeDtypeStruct((bsz * s, vw), F32),
        scratch_shapes=[pltpu.VMEM((tt, kw), F32)] * 3 + [pltpu.VMEM((tt, vw), F32)] * 3
                       + [pltpu.VMEM((kw, vw), F32), pltpu.VMEM((1, kw), F32),
                          pltpu.VMEM((1, vw), F32), pltpu.VMEM((1, kw), F32)],
        compiler_params=_cparams("parallel", "arbitrary"),
        name="mlstm_mixer",
    )(p, p, p, p, p, *consts)


def _unit_lower_inverse(ls, wm_ref, mbd):
    def mm(xs, ys):
        return [_dot(a.astype(BF16), _block_diag(b, mbd)) for a, b in zip(xs, ys)]

    ident = wm_ref[M_IDENT]
    d8 = wm_ref[M_D8]
    l8 = [l * d8 for l in ls]
    l8_2 = mm(l8, l8)
    l8_4 = mm(l8_2, l8_2)
    t = mm(mm([ident - a for a in l8], [ident + a for a in l8_2]), [ident + a for a in l8_4])
    for level in (M_O16, M_O32, M_O64):
        off = wm_ref[level]
        corr = mm(mm(t, [l * off for l in ls]), t)
        t = [a - b for a, b in zip(t, corr)]
    return t


def _gdn_kernel(qkv_ref, z_ref, gb_ref, ga_ref, cw_ref, alog_ref, dtb_ref, nw_ref,
                tri_ref, e_ref, seg_ref, wm_ref, mbd_ref, mbd2_ref,
                o_ref, xbuf, xc, gcw, btw, u0b, wdb, atb, qdb, kdb, ob, sst):
    tt = z_ref.shape[0]
    hw = GDN_HEADS * GDN_DK
    n_groups = hw // GROUP

    @pl.when(pl.program_id(1) == 0)
    def _():
        xbuf[0:SUBLANES, :] = jnp.zeros((SUBLANES, 3 * hw), F32)
        sst[...] = jnp.zeros_like(sst)

    xbuf[SUBLANES:SUBLANES + tt, :] = qkv_ref[...]
    _conv_silu(xbuf, cw_ref, None, xc, tt, 3 * hw)
    xbuf[0:SUBLANES, :] = xbuf[tt:tt + SUBLANES, :]

    for j, s in ((0, GDN_DK ** -0.5), (1, 1.0)):
        sl = slice(j * hw, (j + 1) * hw)
        x = xc[:, sl]
        xc[:, sl] = x * (lax.rsqrt(_sel_cols(x * x, seg_ref, 2) + L2_EPS) * s)

    beta = _sigmoid(gb_ref[...])
    g = -jnp.exp(alog_ref[...]) * _softplus(ga_ref[...] + dtb_ref[...])
    gcw[...] = _sel_cols(_sel_rows(tri_ref, g), e_ref)
    btw[...] = _sel_cols(beta, e_ref)

    ident = wm_ref[M_IDENT]
    causal = wm_ref[M_CAUSAL] > 0.5
    strict = wm_ref[M_STRICT]
    mbd_f = mbd_ref[...]
    mbd = mbd_f.astype(BF16)
    mbd2 = mbd2_ref[...].astype(BF16)
    groups = range(n_groups)
    lanes = [slice(grp * GROUP, (grp + 1) * GROUP) for grp in groups]

    tiles = [(slice(c * CHUNK, (c + 1) * CHUNK), grp) for c in range(tt // CHUNK) for grp in groups]
    kbs, kbds, decs, lowers = [], [], [], []
    for rows, grp in tiles:
        l = lanes[grp]
        gc = gcw[rows, l]
        k = xc[rows, hw + grp * GROUP:hw + (grp + 1) * GROUP]
        g_row = jnp.sum(gc * ident, axis=0, keepdims=True)
        decs.append(jnp.exp(jnp.where(causal, gc - g_row, NEG)))
        kbs.append(k * btw[rows, l])
        kbds.append(_block_diag(k, mbd))
        kdb[rows, l] = k * jnp.exp(gc[CHUNK - 1:CHUNK, :] - gc)
    for kb, kbd, dec in zip(kbs, kbds, decs):
        lowers.append(_dot_nt(kb.astype(BF16), kbd) * dec * strict)
    for (rows, grp), kbd, dec in zip(tiles, kbds, decs):
        q = xc[rows, lanes[grp]]
        atb[rows, lanes[grp]] = _dot_nt(q.astype(BF16), kbd) * dec
    t_invs = _unit_lower_inverse(lowers, wm_ref, mbd)
    rbds = []
    for (rows, grp), kb in zip(tiles, kbs):
        l = lanes[grp]
        eg = jnp.exp(gcw[rows, l])
        v = xc[rows, 2 * hw + grp * GROUP:2 * hw + (grp + 1) * GROUP]
        rbds.append(_block_diag(jnp.concatenate([v * btw[rows, l], kb * eg], axis=1), mbd2))
        qdb[rows, l] = xc[rows, l] * eg
    for (rows, grp), t_inv, rbd in zip(tiles, t_invs, rbds):
        sol = _dot(t_inv.astype(BF16), rbd)
        u0b[rows, lanes[grp]] = sol[:, :GROUP]
        wdb[rows, lanes[grp]] = sol[:, GROUP:]

    def chunk(c, carry):
        r0 = pl.multiple_of(c * CHUNK, CHUNK)
        rows = pl.ds(r0, CHUNK)
        sts = [sst[grp] for grp in groups]
        rs = [_dot(jnp.concatenate([wdb[rows, l], qdb[rows, l]], axis=0).astype(BF16), st.astype(BF16))
              for l, st in zip(lanes, sts)]
        us = [u0b[rows, l] - r[:CHUNK] for l, r in zip(lanes, rs)]
        intra = [_dot(atb[rows, l].astype(BF16), _block_diag(u, mbd)) for l, u in zip(lanes, us)]
        upd = [_dot_tn(kdb[rows, l].astype(BF16), u.astype(BF16)) for l, u in zip(lanes, us)]
        for grp in groups:
            l = lanes[grp]
            ob[rows, l] = rs[grp][CHUNK:] + intra[grp]
            g_last = gcw[pl.ds(r0 + CHUNK - 1, 1), l]
            sst[grp] = sts[grp] * jnp.exp(g_last) + upd[grp] * mbd_f
        return carry

    lax.fori_loop(0, tt // CHUNK, chunk, 0)

    o = ob[...]
    ms = _sel_cols(o * o, seg_ref, 2) * (1.0 / GDN_DV)
    o_ref[...] = o * lax.rsqrt(ms + RMS_EPS) * nw_ref[...] * _silu(z_ref[...])


def _gdn_mixer(p, bsz, s, conv_w, a_log, dt_bias, norm_w):
    tt = TT_MIX
    nt = s // tt
    hw = GDN_HEADS * GDN_DK
    row = lambda b, i: b * nt + i
    gate0 = (p.shape[1] - 4 * LANES) // LANES
    consts = [conv_w.astype(F32), _pad_lanes(a_log), _pad_lanes(dt_bias),
              jnp.tile(norm_w.astype(F32), GDN_HEADS).reshape(1, hw),
              _chunk_tri(tt, True), _expand_mat(GDN_HEADS, GDN_DK), _segment_ones(hw, GDN_DK), _wide_masks(),
              _bd_mask(GROUP, CHUNK), _bd_mask(2 * GROUP, CHUNK)]
    return pl.pallas_call(
        _gdn_kernel,
        grid=(bsz, nt),
        in_specs=[pl.BlockSpec((tt, 3 * hw), lambda b, i: (row(b, i), 0)),
                  pl.BlockSpec((tt, hw), lambda b, i: (row(b, i), 3)),
                  pl.BlockSpec((tt, LANES), lambda b, i: (row(b, i), gate0)),
                  pl.BlockSpec((tt, LANES), lambda b, i: (row(b, i), gate0 + 1))]
                 + [_const_spec(c.shape, 2) for c in consts],
        out_specs=pl.BlockSpec((tt, hw), lambda b, i: (row(b, i), 0)),
        out_shape=jax.ShapeDtypeStruct((bsz * s, hw), F32),
        scratch_shapes=[pltpu.VMEM((tt + SUBLANES, 3 * hw), F32), pltpu.VMEM((tt, 3 * hw), F32)]
                       + [pltpu.VMEM((tt, hw), F32)] * 8
                       + [pltpu.VMEM((hw // GROUP, GROUP, GROUP), F32)],
        compiler_params=_cparams("parallel", "arbitrary"),
        name="gdn_mixer",
    )(p, p, p, p, *consts)


def _pad_cols(w):
    return jnp.pad(w, ((0, 0), (0, LANES - w.shape[1])))


def _hybrid_w_in(w):
    hw = GDN_HEADS * GDN_DK
    a_main = 4 * hw
    b0 = a_main + 2 * GDN_HEADS
    b_main = 2 * MLSTM_HEADS * MLSTM_DK + 2 * MLSTM_HEADS * MLSTM_DV
    return jnp.concatenate([w[:, :a_main], w[:, b0:b0 + b_main],
                            _pad_cols(w[:, a_main:a_main + GDN_HEADS]), _pad_cols(w[:, a_main + GDN_HEADS:b0]),
                            _pad_cols(w[:, b0 + b_main:b0 + b_main + MLSTM_HEADS]),
                            _pad_cols(w[:, b0 + b_main + MLSTM_HEADS:])], axis=1)


def _ssd_w_in(w, n_heads):
    main = w.shape[1] - n_heads
    return jnp.concatenate([w[:, :main], _pad_cols(w[:, main:])], axis=1)


def kernel(x, ffn_pre_w_gate, ffn_pre_w_up, ffn_pre_w_down, ln_pre_g, ln_pre_b, hyb_w_in, hyb_conv_w, gdn_a_log, gdn_dt_bias, gdn_norm_w, mlstm_i_bias, mlstm_f_bias, hyb_w_out, ssd_w_in, ssd_conv_w, ssd_conv_b, ssd_a_log, ssd_dt_bias, ssd_d_skip, ssd_norm_w, ssd_w_out, ln_mix_g, ln_mix_b, ffn_post_w_gate, ffn_post_w_up, ffn_post_w_down, ln_post_g, ln_post_b):
    bsz, s, d = x.shape
    depth = ffn_pre_w_gate.shape[0]
    alpha = (2 * depth) ** 0.25
    gdn_w = GDN_HEADS * GDN_DV
    h = x.reshape(bsz * s, d)
    for l in range(depth):
        h = _ffn(h, ffn_pre_w_gate[l], ffn_pre_w_up[l], ffn_pre_w_down[l], ln_pre_g[l], ln_pre_b[l], alpha)
        j = l // 2
        if l % 2 == 0:
            p = _proj(h, _hybrid_w_in(hyb_w_in[j]))
            o_a = _gdn_mixer(p, bsz, s, hyb_conv_w[j], gdn_a_log[j], gdn_dt_bias[j], gdn_norm_w[j])
            o_b = _mlstm_mixer(p, bsz, s, 4 * gdn_w, mlstm_i_bias[j], mlstm_f_bias[j])
            h = _out_ln([o_a, o_b], [hyb_w_out[j][:gdn_w], hyb_w_out[j][gdn_w:]], h,
                        ln_mix_g[l], ln_mix_b[l], alpha)
        else:
            p = _proj(h, _ssd_w_in(ssd_w_in[j], ssd_a_log.shape[1]))
            y = _ssd_mixer(p, bsz, s, ssd_conv_w[j], ssd_conv_b[j], ssd_a_log[j], ssd_dt_bias[j],
                           ssd_d_skip[j], ssd_norm_w[j])
            h = _out_ln([y], [ssd_w_out[j]], h, ln_mix_g[l], ln_mix_b[l], alpha)
        h = _ffn(h, ffn_post_w_gate[l], ffn_post_w_up[l], ffn_post_w_down[l], ln_post_g[l], ln_post_b[l], alpha)
    return h.reshape(bsz, s, d)
```

```python
import functools

import jax
import jax.numpy as jnp
import numpy as np
from jax import lax
from jax.experimental import pallas as pl
from jax.experimental.pallas import tpu as pltpu

F32 = jnp.float32
BF16 = jnp.bfloat16

LN_EPS = 1e-5
RMS_EPS = 1e-6
L2_EPS = 1e-6
CHUNK = 64
CONV_K = 4
NEG = -1e30

GDN_HEADS, GDN_DK, GDN_DV = 8, 64, 64
MLSTM_HEADS, MLSTM_DK, MLSTM_DV = 4, 64, 128
SSD_HEADDIM, SSD_GROUPS, SSD_STATE = 64, 4, 128

LANES = 128
SUBLANES = 8
VMEM_LIMIT = 56 * 1024 * 1024
GROUP = 4 * CHUNK

TM_DENSE = 512
TT_MIX = 256


def _cparams(*sem):
    return pltpu.CompilerParams(dimension_semantics=sem, vmem_limit_bytes=VMEM_LIMIT)


def _dot(a, b):
    return jnp.dot(a, b, preferred_element_type=F32)


def _dot_nt(a, b):
    return lax.dot_general(a, b, (((1,), (1,)), ((), ())), preferred_element_type=F32)


def _dot_tn(a, b):
    return lax.dot_general(a, b, (((0,), (0,)), ((), ())), preferred_element_type=F32)


def _layer_norm(y, g, b):
    mu = jnp.mean(y, -1, keepdims=True)
    yc = y - mu
    var = jnp.mean(yc * yc, -1, keepdims=True)
    return yc * lax.rsqrt(var + LN_EPS) * g + b


def _sigmoid(x):
    return 1.0 / (1.0 + jnp.exp(-x))


def _silu(x):
    return x * _sigmoid(x)


def _softplus(x):
    return jnp.maximum(x, 0.0) + jnp.log(1.0 + jnp.exp(-jnp.abs(x)))


def _split(x, parts):
    out = []
    r = x
    for p in range(parts):
        h = r.astype(BF16)
        out.append(h)
        if p + 1 < parts:
            r = r - h.astype(F32)
    return out


def _sel_rows(m01x3_ref, x):
    return _dot(m01x3_ref[...], jnp.concatenate(_split(x, 3), axis=0))


def _sel_cols(x, m01x3_ref, parts=3):
    return _dot(jnp.concatenate(_split(x, parts), axis=1), m01x3_ref[...])


def _tile4(x):
    return jnp.concatenate([x, x, x, x], axis=0)


def _block_diag(x, mask_bf16):
    return _tile4(x.astype(BF16)) * mask_bf16


def _ffn_kernel(x_ref, wg_ref, wu_ref, wd_ref, g_ref, b_ref, o_ref, *, fc, alpha):
    x = x_ref[...]
    xb = x.astype(BF16)
    acc = jnp.zeros_like(x)
    for c in range(wg_ref.shape[1] // fc):
        sl = slice(c * fc, (c + 1) * fc)
        hg = _dot(xb, wg_ref[:, sl])
        hu = _dot(xb, wu_ref[:, sl])
        acc = acc + _dot((_silu(hg) * hu).astype(BF16), wd_ref[sl, :])
    o_ref[...] = _layer_norm(alpha * x + 0.5 * acc, g_ref[...], b_ref[...])


def _const_spec(shape, grid_rank=1):
    zeros = (0,) * len(shape)
    if grid_rank == 1:
        return pl.BlockSpec(shape, lambda i: zeros, pipeline_mode=pl.Buffered(1))
    return pl.BlockSpec(shape, lambda b, i: zeros, pipeline_mode=pl.Buffered(1))


def _ffn(x, wg, wu, wd, g, b, alpha):
    t, d = x.shape
    f = wg.shape[1]
    tm = TM_DENSE
    return pl.pallas_call(
        functools.partial(_ffn_kernel, fc=2 * LANES, alpha=alpha),
        grid=(t // tm,),
        in_specs=[pl.BlockSpec((tm, d), lambda i: (i, 0)),
                  _const_spec((d, f)), _const_spec((d, f)), _const_spec((f, d)),
                  _const_spec((1, d)), _const_spec((1, d))],
        out_specs=pl.BlockSpec((tm, d), lambda i: (i, 0)),
        out_shape=jax.ShapeDtypeStruct((t, d), F32),
        compiler_params=_cparams("parallel"),
        name="ffn",
    )(x, wg.astype(BF16), wu.astype(BF16), wd.astype(BF16), g.reshape(1, d), b.reshape(1, d))


def _chunk_tri(tt, inclusive_lower):
    i = np.arange(tt)[:, None]
    j = np.arange(tt)[None, :]
    same = (i // CHUNK) == (j // CHUNK)
    m = same & (j <= i) if inclusive_lower else same
    return jnp.asarray(np.tile(m.astype(np.float32), (1, 3)), BF16)


def _expand_mat(n_heads, width, parts=3):
    e = np.zeros((LANES, n_heads * width), np.float32)
    for h in range(n_heads):
        e[h, h * width:(h + 1) * width] = 1.0
    return jnp.asarray(np.tile(e, (parts, 1)), BF16)


def _segment_ones(n, seg, parts=2):
    i = np.arange(n)
    m = (i[:, None] // seg == i[None, :] // seg).astype(np.float32)
    return jnp.asarray(np.tile(m, (parts, 1)), BF16)


def _wide_masks():
    i = np.arange(CHUNK)[:, None]
    j = (np.arange(GROUP) % CHUNK)[None, :]
    ident = i == j
    causal = i >= j
    strict = i > j
    d8 = (i // 8) == (j // 8)
    o16 = ((i // 16) == (j // 16)) & ((i // 8) == (j // 8) + 1)
    o32 = ((i // 32) == (j // 32)) & ((i // 16) == (j // 16) + 1)
    o64 = (i // 32) == (j // 32) + 1
    return jnp.asarray(np.stack([ident, causal, strict, d8, o16, o32, o64]).astype(np.float32))


M_IDENT, M_CAUSAL, M_STRICT, M_D8, M_O16, M_O32, M_O64 = range(7)


def _bd_mask(n_cols, col_block):
    r = np.arange(GROUP)[:, None] // CHUNK
    c = (np.arange(n_cols)[None, :] % (4 * col_block)) // col_block
    return jnp.asarray((r == c).astype(np.float32))


def _head_masks(n_heads, width):
    m = np.zeros((SUBLANES, n_heads * width), np.float32)
    for h in range(n_heads):
        m[h, h * width:(h + 1) * width] = 1.0
    return jnp.asarray(m)


def _pad_cols(w):
    return jnp.pad(w, ((0, 0), (0, LANES - w.shape[1])))


def _pad_lanes(v):
    v = v.reshape(1, -1).astype(F32)
    return jnp.pad(v, ((0, 0), (0, LANES - v.shape[1])))


def _project_conv_silu(xb, w_ref, col0, xbuf, cw_ref, cb_ref, dst, tt, width):
    for start in range(0, width, 4 * LANES):
        sl = slice(start, min(start + 4 * LANES, width))
        xbuf[SUBLANES:SUBLANES + tt, sl] = _dot(xb, w_ref[:, col0 + sl.start:col0 + sl.stop])
        acc = cw_ref[0:1, sl] * xbuf[5:5 + tt, sl]
        for k in range(1, CONV_K):
            acc = acc + cw_ref[k:k + 1, sl] * xbuf[5 + k:5 + k + tt, sl]
        if cb_ref is not None:
            acc = acc + cb_ref[:, sl]
        dst[:, sl] = _silu(acc)
    xbuf[0:SUBLANES, :] = xbuf[tt:tt + SUBLANES, :]


def _tile_spec(tt, width, nt):
    return pl.BlockSpec((tt, width), lambda b, i: (b * nt + i, 0))


def _ssd_kernel(h_ref, win_ref, wout_ref, lg_ref, lb_ref,
                cw_ref, cb_ref, alog_ref, dtb_ref, dskip_ref, nw_ref, tri_ref, e_ref, wm_ref, mbd_ref,
                o_ref, zbuf, xbuf, xc, acol, xdt, hst, *, alpha):
    tt = h_ref.shape[0]
    ncx = zbuf.shape[1]
    ncb = xbuf.shape[1] - ncx
    gw = ncx // SSD_GROUPS

    @pl.when(pl.program_id(1) == 0)
    def _():
        xbuf[0:SUBLANES, :] = jnp.zeros((SUBLANES, ncx + ncb), F32)
        hst[...] = jnp.zeros_like(hst)

    hb = h_ref[...].astype(BF16)
    _project_conv_silu(hb, win_ref, ncx, xbuf, cw_ref, cb_ref, xc, tt, ncx + ncb)

    dt = _softplus(_dot(hb, win_ref[:, 2 * ncx + ncb:]) + dtb_ref[...])
    a_cum = _sel_rows(tri_ref, dt * (-jnp.exp(alog_ref[...])))
    for start in range(0, ncx, 4 * LANES):
        sl = slice(start, start + 4 * LANES)
        zbuf[:, sl] = _dot(hb, win_ref[:, sl])
        acol[:, sl] = _sel_cols(a_cum, e_ref.at[:, sl])
        xdt[:, sl] = xc[:, sl] * _sel_cols(dt, e_ref.at[0:2 * LANES, sl], 2)

    ident = wm_ref[M_IDENT]
    causal = wm_ref[M_CAUSAL] > 0.5
    mbd = mbd_ref[...].astype(BF16)
    groups = range(SSD_GROUPS)
    halves = range(gw // GROUP)

    def chunk(c, carry):
        r0 = pl.multiple_of(c * CHUNK, CHUNK)
        rows = pl.ds(r0, CHUNK)
        gls = [slice(g * gw, (g + 1) * gw) for g in groups]
        b_gs = [xc[rows, ncx + g * SSD_STATE:ncx + (g + 1) * SSD_STATE].astype(BF16) for g in groups]
        c_gs = [xc[rows, ncx + ncb // 2 + g * SSD_STATE:ncx + ncb // 2 + (g + 1) * SSD_STATE].astype(BF16)
                for g in groups]
        cb_wide = [_dot_nt(c_gs[g], _tile4(b_gs[g])) for g in groups]
        a_gs = [acol[rows, gl] for gl in gls]
        a_lasts = [acol[pl.ds(r0 + CHUNK - 1, 1), gl] for gl in gls]
        x_gs = [xdt[rows, gl] for gl in gls]
        lhs, rhs = [], []
        for g in groups:
            for hh in halves:
                hl = slice(hh * GROUP, (hh + 1) * GROUP)
                a_c = a_gs[g][:, hl]
                a_row = jnp.sum(a_c * ident, axis=0, keepdims=True)
                seg = jnp.exp(jnp.where(causal, a_c - a_row, NEG))
                lhs.append((cb_wide[g] * seg).astype(BF16))
                rhs.append(_block_diag(x_gs[g][:, hl], mbd))
        y_diag = [_dot(a, b) for a, b in zip(lhs, rhs)]
        h_prev = [hst[g] for g in groups]
        y_off = [_dot(c_gs[g], h_prev[g].astype(BF16)) for g in groups]
        upd = [_dot_tn(b_gs[g], (x_gs[g] * jnp.exp(a_lasts[g] - a_gs[g])).astype(BF16)) for g in groups]
        for g in groups:
            gl = gls[g]
            hst[g] = h_prev[g] * jnp.exp(a_lasts[g]) + upd[g]
            y = jnp.concatenate(y_diag[g * len(halves):(g + 1) * len(halves)], axis=1)
            y = y + y_off[g] * jnp.exp(a_gs[g]) + dskip_ref[:, gl] * xc[rows, gl]
            y = y * _silu(zbuf[rows, gl])
            xdt[rows, gl] = y * lax.rsqrt(jnp.mean(y * y, -1, keepdims=True) + RMS_EPS) * nw_ref[:, gl]
        return carry

    lax.fori_loop(0, tt // CHUNK, chunk, 0)
    acc = alpha * h_ref[...] + _dot(xdt[...].astype(BF16), wout_ref[...])
    o_ref[...] = _layer_norm(acc, lg_ref[...], lb_ref[...])


def _ssd_layer(h, bsz, s, w_in, conv_w, conv_b, a_log, dt_bias, d_skip, norm_w, w_out, ln_g, ln_b, alpha):
    tt = TT_MIX
    nt = s // tt
    d = h.shape[1]
    n_heads = a_log.shape[0]
    ncx = n_heads * SSD_HEADDIM
    ncb = 2 * SSD_GROUPS * SSD_STATE
    w_in = jnp.concatenate([w_in[:, :2 * ncx + ncb], _pad_cols(w_in[:, 2 * ncx + ncb:])], axis=1).astype(BF16)
    consts = [w_in, w_out.astype(BF16), ln_g.reshape(1, d), ln_b.reshape(1, d),
              conv_w.astype(F32), conv_b.reshape(1, -1).astype(F32), _pad_lanes(a_log), _pad_lanes(dt_bias),
              jnp.repeat(d_skip.astype(F32), SSD_HEADDIM).reshape(1, ncx), norm_w.reshape(1, ncx).astype(F32),
              _chunk_tri(tt, True), _expand_mat(n_heads, SSD_HEADDIM), _wide_masks(), _bd_mask(GROUP, CHUNK)]
    return pl.pallas_call(
        functools.partial(_ssd_kernel, alpha=alpha),
        grid=(bsz, nt),
        in_specs=[_tile_spec(tt, d, nt)] + [_const_spec(c.shape, 2) for c in consts],
        out_specs=_tile_spec(tt, d, nt),
        out_shape=jax.ShapeDtypeStruct((bsz * s, d), F32),
        scratch_shapes=[pltpu.VMEM((tt, ncx), F32),
                        pltpu.VMEM((tt + SUBLANES, ncx + ncb), F32), pltpu.VMEM((tt, ncx + ncb), F32),
                        pltpu.VMEM((tt, ncx), F32), pltpu.VMEM((tt, ncx), F32),
                        pltpu.VMEM((SSD_GROUPS, SSD_STATE, ncx // SSD_GROUPS), F32)],
        compiler_params=_cparams("parallel", "arbitrary"),
        name="ssd_layer",
    )(h, *consts)


def _per_head(x, hm_in, hm_outs, reduce_fn, fill):
    outs = [0.0] * len(hm_outs)
    for h in range(MLSTM_HEADS):
        r = reduce_fn(jnp.where(hm_in[h:h + 1, :] > 0.5, x, fill), axis=-1, keepdims=True)
        outs = [o + r * hm[h:h + 1, :] for o, hm in zip(outs, hm_outs)]
    return outs


def _mlstm_kernel(h_ref, oa_ref, win_ref, woa_ref, wob_ref, lg_ref, lb_ref, ib_ref, fb_ref,
                  tri_ref, blk_ref, e2_ref, e5_ref, wm_ref, mbd_ref, mbdv_ref, hm2_ref, hm5_ref,
                  o_ref, qkvo, ob, b2, li2, bt2, b5, li5, bt5, cst, nst, m5s, m2s, *, alpha):
    tt = h_ref.shape[0]
    kw = MLSTM_HEADS * MLSTM_DK
    vw = MLSTM_HEADS * MLSTM_DV
    scale = MLSTM_DK ** -0.5

    @pl.when(pl.program_id(1) == 0)
    def _():
        cst[...] = jnp.zeros_like(cst)
        nst[...] = jnp.zeros_like(nst)
        m5s[...] = jnp.zeros_like(m5s)
        m2s[...] = jnp.zeros_like(m2s)

    hb = h_ref[...].astype(BF16)
    gates = _dot(hb, win_ref[:, 2 * kw + 2 * vw:])
    log_i = gates[:, 0:LANES] + ib_ref[...]
    log_f = -_softplus(-(gates[:, LANES:] + fb_ref[...]))
    for start in range(0, 2 * kw + 2 * vw, 4 * LANES):
        sl = slice(start, start + 4 * LANES)
        qkvo[:, sl] = _dot(hb, win_ref[:, sl])

    b = _sel_rows(tri_ref, log_f)
    bt = _sel_rows(blk_ref, log_f)
    for src, d2, d5 in ((b, b2, b5), (log_i, li2, li5), (bt, bt2, bt5)):
        d2[...] = _sel_cols(src, e2_ref)
        d5[...] = _sel_cols(src, e5_ref)

    ident = wm_ref[M_IDENT]
    causal = wm_ref[M_CAUSAL] > 0.5
    mbd = mbd_ref[...].astype(BF16)
    mbdv_f = mbdv_ref[...]
    mbdv = mbdv_f.astype(BF16)
    hm2 = hm2_ref[...]
    hm5 = hm5_ref[...]
    chunks = [slice(c * CHUNK, (c + 1) * CHUNK) for c in range(tt // CHUNK)]

    qs = [qkvo[rows, 0:kw] for rows in chunks]
    ks = [qkvo[rows, kw:2 * kw] * scale for rows in chunks]
    vs = [qkvo[rows, 2 * kw:2 * kw + vw] for rows in chunks]
    d_logs, mi2s, mi5s, me2s, me5s, k_ends = [], [], [], [], [], []
    for rows, k in zip(chunks, ks):
        bc2, lic2 = b2[rows, :], li2[rows, :]
        row_v = jnp.sum((lic2 - bc2) * ident, axis=0, keepdims=True)
        d_log = jnp.where(causal, bc2 + row_v, NEG)
        mi2, mi5 = _per_head(d_log, hm2, (hm2, hm5), jnp.max, NEG)
        d_logs.append(d_log)
        mi2s.append(mi2)
        mi5s.append(mi5)
        w2 = bt2[rows, :] - bc2 + lic2
        me2 = jnp.max(w2, axis=0, keepdims=True)
        me2s.append(me2)
        me5s.append(jnp.max(bt5[rows, :] - b5[rows, :] + li5[rows, :], axis=0, keepdims=True))
        k_ends.append(k * jnp.exp(w2 - me2))
    s_qks = [_dot_nt(q.astype(BF16), _block_diag(k, mbd)) * jnp.exp(d_log - mi2)
             for q, k, d_log, mi2 in zip(qs, ks, d_logs, mi2s)]
    c_ends = [_dot_tn(k_end.astype(BF16), v.astype(BF16)) * mbdv_f for k_end, v in zip(k_ends, vs)]
    num_is = [_dot(s_qk.astype(BF16), _block_diag(v, mbdv)) for s_qk, v in zip(s_qks, vs)]
    den_is = [_per_head(s_qk, hm2, (hm5,), jnp.sum, 0.0)[0] for s_qk in s_qks]

    c_st, n_st, m5, m2 = cst[...], nst[...], m5s[...], m2s[...]
    c_sts, n_sts, m5l = [], [], []
    for rows, c_end, k_end, me5, me2 in zip(chunks, c_ends, k_ends, me5s, me2s):
        c_sts.append(c_st)
        n_sts.append(n_st)
        m5l.append(m5)
        bl5, bl2 = bt5[rows.start:rows.start + 1, :], bt2[rows.start:rows.start + 1, :]
        mn5 = jnp.maximum(bl5 + m5, me5)
        mn2 = jnp.maximum(bl2 + m2, me2)
        c_st = c_st * jnp.exp(bl5 + m5 - mn5) + c_end * jnp.exp(me5 - mn5)
        n_st = n_st * jnp.exp(bl2 + m2 - mn2) + jnp.sum(k_end, axis=0, keepdims=True) * jnp.exp(me2 - mn2)
        m5, m2 = mn5, mn2
    cst[...], nst[...], m5s[...], m2s[...] = c_st, n_st, m5, m2
    q_cs = [_dot(q.astype(BF16), c.astype(BF16)) for q, c in zip(qs, c_sts)]
    for rows, q, q_c, n, m, mi5, num_i, den_i in zip(chunks, qs, q_cs, n_sts, m5l, mi5s, num_is, den_is):
        a5 = b5[rows, :] + m
        m_t = jnp.maximum(a5, mi5)
        ea, ei = jnp.exp(a5 - m_t), jnp.exp(mi5 - m_t)
        (qn,) = _per_head(q * n, hm2, (hm5,), jnp.sum, 0.0)
        h = (ea * q_c + ei * num_i) / jnp.maximum(jnp.abs(ea * qn + ei * den_i), jnp.exp(-m_t))
        ob[rows, :] = _sigmoid(qkvo[rows, 2 * kw + vw:]) * h

    acc = (alpha * h_ref[...] + _dot(oa_ref[...].astype(BF16), woa_ref[...])
           + _dot(ob[...].astype(BF16), wob_ref[...]))
    o_ref[...] = _layer_norm(acc, lg_ref[...], lb_ref[...])


def _mlstm_layer(h, o_a, bsz, s, w_in, i_bias, f_bias, w_out_a, w_out_b, ln_g, ln_b, alpha):
    tt = TT_MIX
    nt = s // tt
    d = h.shape[1]
    kw = MLSTM_HEADS * MLSTM_DK
    vw = MLSTM_HEADS * MLSTM_DV
    main = 2 * kw + 2 * vw
    w_in = jnp.concatenate([w_in[:, :main], _pad_cols(w_in[:, main:main + MLSTM_HEADS]),
                            _pad_cols(w_in[:, main + MLSTM_HEADS:])], axis=1).astype(BF16)
    consts = [w_in, w_out_a.astype(BF16), w_out_b.astype(BF16), ln_g.reshape(1, d), ln_b.reshape(1, d),
              _pad_lanes(i_bias), _pad_lanes(f_bias), _chunk_tri(tt, True), _chunk_tri(tt, False),
              _expand_mat(MLSTM_HEADS, MLSTM_DK), _expand_mat(MLSTM_HEADS, MLSTM_DV), _wide_masks(),
              _bd_mask(kw, MLSTM_DK), _bd_mask(vw, MLSTM_DV),
              _head_masks(MLSTM_HEADS, MLSTM_DK), _head_masks(MLSTM_HEADS, MLSTM_DV)]
    return pl.pallas_call(
        functools.partial(_mlstm_kernel, alpha=alpha),
        grid=(bsz, nt),
        in_specs=[_tile_spec(tt, d, nt), _tile_spec(tt, o_a.shape[1], nt)]
                 + [_const_spec(c.shape, 2) for c in consts],
        out_specs=_tile_spec(tt, d, nt),
        out_shape=jax.ShapeDtypeStruct((bsz * s, d), F32),
        scratch_shapes=[pltpu.VMEM((tt, main), F32), pltpu.VMEM((tt, vw), F32)]
                       + [pltpu.VMEM((tt, kw), F32)] * 3 + [pltpu.VMEM((tt, vw), F32)] * 3
                       + [pltpu.VMEM((kw, vw), F32), pltpu.VMEM((1, kw), F32),
                          pltpu.VMEM((1, vw), F32), pltpu.VMEM((1, kw), F32)],
        compiler_params=_cparams("parallel", "arbitrary"),
        name="mlstm_layer",
    )(h, o_a, *consts)


def _unit_lower_inverse(ls, wm_ref, mbd):
    def mm(xs, ys):
        return [_dot(a.astype(BF16), _block_diag(b, mbd)) for a, b in zip(xs, ys)]

    ident = wm_ref[M_IDENT]
    d8 = wm_ref[M_D8]
    l8 = [l * d8 for l in ls]
    l8_2 = mm(l8, l8)
    l8_4 = mm(l8_2, l8_2)
    t = mm(mm([ident - a for a in l8], [ident + a for a in l8_2]), [ident + a for a in l8_4])
    for level in (M_O16, M_O32, M_O64):
        off = wm_ref[level]
        corr = mm(mm(t, [l * off for l in ls]), t)
        t = [a - b for a, b in zip(t, corr)]
    return t


def _gdn_kernel(h_ref, win_ref, cw_ref, alog_ref, dtb_ref, nw_ref,
                tri_ref, e_ref, seg_ref, wm_ref, mbd_ref, mbd2_ref,
                o_ref, zbuf, xbuf, xc, gcw, btw, u0b, wdb, atb, qdb, kdb, ob, sst):
    tt = h_ref.shape[0]
    hw = GDN_HEADS * GDN_DK
    n_groups = hw // GROUP

    @pl.when(pl.program_id(1) == 0)
    def _():
        xbuf[0:SUBLANES, :] = jnp.zeros((SUBLANES, 3 * hw), F32)
        sst[...] = jnp.zeros_like(sst)

    hb = h_ref[...].astype(BF16)
    _project_conv_silu(hb, win_ref, 0, xbuf, cw_ref, None, xc, tt, 3 * hw)
    gates = _dot(hb, win_ref[:, 4 * hw:])
    beta = _sigmoid(gates[:, 0:LANES])
    g = -jnp.exp(alog_ref[...]) * _softplus(gates[:, LANES:] + dtb_ref[...])
    zbuf[...] = _dot(hb, win_ref[:, 3 * hw:4 * hw])

    for j, s in ((0, GDN_DK ** -0.5), (1, 1.0)):
        sl = slice(j * hw, (j + 1) * hw)
        x = xc[:, sl]
        xc[:, sl] = x * (lax.rsqrt(_sel_cols(x * x, seg_ref, 2) + L2_EPS) * s)

    gcw[...] = _sel_cols(_sel_rows(tri_ref, g), e_ref)
    btw[...] = _sel_cols(beta, e_ref)

    ident = wm_ref[M_IDENT]
    causal = wm_ref[M_CAUSAL] > 0.5
    strict = wm_ref[M_STRICT]
    mbd_f = mbd_ref[...]
    mbd = mbd_f.astype(BF16)
    mbd2 = mbd2_ref[...].astype(BF16)
    groups = range(n_groups)
    lanes = [slice(grp * GROUP, (grp + 1) * GROUP) for grp in groups]

    tiles = [(slice(c * CHUNK, (c + 1) * CHUNK), grp) for c in range(tt // CHUNK) for grp in groups]
    kbs, kbds, decs, lowers = [], [], [], []
    for rows, grp in tiles:
        l = lanes[grp]
        gc = gcw[rows, l]
        k = xc[rows, hw + grp * GROUP:hw + (grp + 1) * GROUP]
        g_row = jnp.sum(gc * ident, axis=0, keepdims=True)
        decs.append(jnp.exp(jnp.where(causal, gc - g_row, NEG)))
        kbs.append(k * btw[rows, l])
        kbds.append(_block_diag(k, mbd))
        kdb[rows, l] = k * jnp.exp(gc[CHUNK - 1:CHUNK, :] - gc)
    for kb, kbd, dec in zip(kbs, kbds, decs):
        lowers.append(_dot_nt(kb.astype(BF16), kbd) * dec * strict)
    for (rows, grp), kbd, dec in zip(tiles, kbds, decs):
        q = xc[rows, lanes[grp]]
        atb[rows, lanes[grp]] = _dot_nt(q.astype(BF16), kbd) * dec
    t_invs = _unit_lower_inverse(lowers, wm_ref, mbd)
    rbds = []
    for (rows, grp), kb in zip(tiles, kbs):
        l = lanes[grp]
        eg = jnp.exp(gcw[rows, l])
        v = xc[rows, 2 * hw + grp * GROUP:2 * hw + (grp + 1) * GROUP]
        rbds.append(_block_diag(jnp.concatenate([v * btw[rows, l], kb * eg], axis=1), mbd2))
        qdb[rows, l] = xc[rows, l] * eg
    for (rows, grp), t_inv, rbd in zip(tiles, t_invs, rbds):
        sol = _dot(t_inv.astype(BF16), rbd)
        u0b[rows, lanes[grp]] = sol[:, :GROUP]
        wdb[rows, lanes[grp]] = sol[:, GROUP:]

    def chunk(c, carry):
        r0 = pl.multiple_of(c * CHUNK, CHUNK)
        rows = pl.ds(r0, CHUNK)
        sts = [sst[grp] for grp in groups]
        rs = [_dot(jnp.concatenate([wdb[rows, l], qdb[rows, l]], axis=0).astype(BF16), st.astype(BF16))
              for l, st in zip(lanes, sts)]
        us = [u0b[rows, l] - r[:CHUNK] for l, r in zip(lanes, rs)]
        intra = [_dot(atb[rows, l].astype(BF16), _block_diag(u, mbd)) for l, u in zip(lanes, us)]
        upd = [_dot_tn(kdb[rows, l].astype(BF16), u.astype(BF16)) for l, u in zip(lanes, us)]
        for grp in groups:
            l = lanes[grp]
            ob[rows, l] = rs[grp][CHUNK:] + intra[grp]
            g_last = gcw[pl.ds(r0 + CHUNK - 1, 1), l]
            sst[grp] = sts[grp] * jnp.exp(g_last) + upd[grp] * mbd_f
        return carry

    lax.fori_loop(0, tt // CHUNK, chunk, 0)

    o = ob[...]
    ms = _sel_cols(o * o, seg_ref, 2) * (1.0 / GDN_DV)
    o_ref[...] = o * lax.rsqrt(ms + RMS_EPS) * nw_ref[...] * _silu(zbuf[...])


def _gdn_mixer(h, bsz, s, w_in, conv_w, a_log, dt_bias, norm_w):
    tt = TT_MIX
    nt = s // tt
    d = h.shape[1]
    hw = GDN_HEADS * GDN_DK
    w_in = jnp.concatenate([w_in[:, :4 * hw], _pad_cols(w_in[:, 4 * hw:4 * hw + GDN_HEADS]),
                            _pad_cols(w_in[:, 4 * hw + GDN_HEADS:])], axis=1).astype(BF16)
    consts = [w_in, conv_w.astype(F32), _pad_lanes(a_log), _pad_lanes(dt_bias),
              jnp.tile(norm_w.astype(F32), GDN_HEADS).reshape(1, hw),
              _chunk_tri(tt, True), _expand_mat(GDN_HEADS, GDN_DK), _segment_ones(hw, GDN_DK), _wide_masks(),
              _bd_mask(GROUP, CHUNK), _bd_mask(2 * GROUP, CHUNK)]
    return pl.pallas_call(
        _gdn_kernel,
        grid=(bsz, nt),
        in_specs=[_tile_spec(tt, d, nt)] + [_const_spec(c.shape, 2) for c in consts],
        out_specs=_tile_spec(tt, hw, nt),
        out_shape=jax.ShapeDtypeStruct((bsz * s, hw), F32),
        scratch_shapes=[pltpu.VMEM((tt, hw), F32),
                        pltpu.VMEM((tt + SUBLANES, 3 * hw), F32), pltpu.VMEM((tt, 3 * hw), F32)]
                       + [pltpu.VMEM((tt, hw), F32)] * 8
                       + [pltpu.VMEM((hw // GROUP, GROUP, GROUP), F32)],
        compiler_params=_cparams("parallel", "arbitrary"),
        name="gdn_mixer",
    )(h, *consts)


def kernel(x, ffn_pre_w_gate, ffn_pre_w_up, ffn_pre_w_down, ln_pre_g, ln_pre_b, hyb_w_in, hyb_conv_w, gdn_a_log, gdn_dt_bias, gdn_norm_w, mlstm_i_bias, mlstm_f_bias, hyb_w_out, ssd_w_in, ssd_conv_w, ssd_conv_b, ssd_a_log, ssd_dt_bias, ssd_d_skip, ssd_norm_w, ssd_w_out, ln_mix_g, ln_mix_b, ffn_post_w_gate, ffn_post_w_up, ffn_post_w_down, ln_post_g, ln_post_b):
    bsz, s, d = x.shape
    depth = ffn_pre_w_gate.shape[0]
    alpha = (2 * depth) ** 0.25
    gdn_in = 4 * GDN_HEADS * GDN_DK + 2 * GDN_HEADS
    gdn_out = GDN_HEADS * GDN_DV
    h = x.reshape(bsz * s, d)
    for l in range(depth):
        h = _ffn(h, ffn_pre_w_gate[l], ffn_pre_w_up[l], ffn_pre_w_down[l], ln_pre_g[l], ln_pre_b[l], alpha)
        j = l // 2
        if l % 2 == 0:
            o_a = _gdn_mixer(h, bsz, s, hyb_w_in[j][:, :gdn_in], hyb_conv_w[j], gdn_a_log[j], gdn_dt_bias[j],
                             gdn_norm_w[j])
            h = _mlstm_layer(h, o_a, bsz, s, hyb_w_in[j][:, gdn_in:], mlstm_i_bias[j], mlstm_f_bias[j],
                             hyb_w_out[j][:gdn_out], hyb_w_out[j][gdn_out:], ln_mix_g[l], ln_mix_b[l], alpha)
        else:
            h = _ssd_layer(h, bsz, s, ssd_w_in[j], ssd_conv_w[j], ssd_conv_b[j], ssd_a_log[j], ssd_dt_bias[j],
                           ssd_d_skip[j], ssd_norm_w[j], ssd_w_out[j], ln_mix_g[l], ln_mix_b[l], alpha)
        h = _ffn(h, ffn_post_w_gate[l], ffn_post_w_up[l], ffn_post_w_down[l], ln_post_g[l], ln_post_b[l], alpha)
    return h.reshape(bsz, s, d)
```

```python
import functools

import jax
import jax.numpy as jnp
import numpy as np
from jax import lax
from jax.experimental import pallas as pl
from jax.experimental.pallas import tpu as pltpu

F32 = jnp.float32
BF16 = jnp.bfloat16

LN_EPS = 1e-5
RMS_EPS = 1e-6
L2_EPS = 1e-6
CHUNK = 64
CONV_K = 4
NEG = -1e30

GDN_HEADS, GDN_DK, GDN_DV = 8, 64, 64
MLSTM_HEADS, MLSTM_DK, MLSTM_DV = 4, 64, 128
SSD_HEADDIM, SSD_GROUPS, SSD_STATE = 64, 4, 128

LANES = 128
SUBLANES = 8
VMEM_LIMIT = 56 * 1024 * 1024
GROUP = 4 * CHUNK

TM_DENSE = 512
TT_MIX = 256


def _cparams(*sem):
    return pltpu.CompilerParams(dimension_semantics=sem, vmem_limit_bytes=VMEM_LIMIT)


def _dot(a, b):
    return jnp.dot(a, b, preferred_element_type=F32)


def _dot_nt(a, b):
    return lax.dot_general(a, b, (((1,), (1,)), ((), ())), preferred_element_type=F32)


def _dot_tn(a, b):
    return lax.dot_general(a, b, (((0,), (0,)), ((), ())), preferred_element_type=F32)


def _layer_norm(y, g, b):
    mu = jnp.mean(y, -1, keepdims=True)
    yc = y - mu
    var = jnp.mean(yc * yc, -1, keepdims=True)
    return yc * lax.rsqrt(var + LN_EPS) * g + b


def _sigmoid(x):
    return 1.0 / (1.0 + jnp.exp(-x))


def _silu(x):
    return x * _sigmoid(x)


def _softplus(x):
    return jnp.maximum(x, 0.0) + jnp.log(1.0 + jnp.exp(-jnp.abs(x)))


def _split(x, parts):
    out = []
    r = x
    for p in range(parts):
        h = r.astype(BF16)
        out.append(h)
        if p + 1 < parts:
            r = r - h.astype(F32)
    return out


def _sel_rows(m01x3_ref, x):
    return _dot(m01x3_ref[...], jnp.concatenate(_split(x, 3), axis=0))


def _sel_cols(x, m01x3_ref, parts=3):
    return _dot(jnp.concatenate(_split(x, parts), axis=1), m01x3_ref[...])


def _tile4(x):
    return jnp.concatenate([x, x, x, x], axis=0)


def _block_diag(x, mask_bf16):
    return _tile4(x.astype(BF16)) * mask_bf16


def _ffn_kernel(x_ref, wg_ref, wu_ref, wd_ref, g_ref, b_ref, o_ref, *, fc, alpha):
    x = x_ref[...]
    xb = x.astype(BF16)
    acc = jnp.zeros_like(x)
    for c in range(wg_ref.shape[1] // fc):
        sl = slice(c * fc, (c + 1) * fc)
        hg = _dot(xb, wg_ref[:, sl])
        hu = _dot(xb, wu_ref[:, sl])
        acc = acc + _dot((_silu(hg) * hu).astype(BF16), wd_ref[sl, :])
    o_ref[...] = _layer_norm(alpha * x + 0.5 * acc, g_ref[...], b_ref[...])


def _const_spec(shape, grid_rank=1):
    zeros = (0,) * len(shape)
    if grid_rank == 1:
        return pl.BlockSpec(shape, lambda i: zeros, pipeline_mode=pl.Buffered(1))
    return pl.BlockSpec(shape, lambda b, i: zeros, pipeline_mode=pl.Buffered(1))


def _ffn(x, wg, wu, wd, g, b, layer, alpha):
    t, d = x.shape
    f = wg.shape[2]
    tm = TM_DENSE

    def layer_spec(rows, cols):
        return pl.BlockSpec((None, rows, cols), lambda i: (layer, 0, 0), pipeline_mode=pl.Buffered(1))

    return pl.pallas_call(
        functools.partial(_ffn_kernel, fc=2 * LANES, alpha=alpha),
        grid=(t // tm,),
        in_specs=[pl.BlockSpec((tm, d), lambda i: (i, 0)),
                  layer_spec(d, f), layer_spec(d, f), layer_spec(f, d),
                  _const_spec((1, d)), _const_spec((1, d))],
        out_specs=pl.BlockSpec((tm, d), lambda i: (i, 0)),
        out_shape=jax.ShapeDtypeStruct((t, d), F32),
        compiler_params=_cparams("parallel"),
        name="ffn",
    )(x, wg, wu, wd, g[layer].reshape(1, d), b[layer].reshape(1, d))


def _chunk_tri(tt, inclusive_lower):
    i = np.arange(tt)[:, None]
    j = np.arange(tt)[None, :]
    same = (i // CHUNK) == (j // CHUNK)
    m = same & (j <= i) if inclusive_lower else same
    return jnp.asarray(np.tile(m.astype(np.float32), (1, 3)), BF16)


def _expand_mat(n_heads, width, parts=3):
    e = np.zeros((LANES, n_heads * width), np.float32)
    for h in range(n_heads):
        e[h, h * width:(h + 1) * width] = 1.0
    return jnp.asarray(np.tile(e, (parts, 1)), BF16)


def _segment_ones(n, seg, parts=2):
    i = np.arange(n)
    m = (i[:, None] // seg == i[None, :] // seg).astype(np.float32)
    return jnp.asarray(np.tile(m, (parts, 1)), BF16)


def _wide_masks():
    i = np.arange(CHUNK)[:, None]
    j = (np.arange(GROUP) % CHUNK)[None, :]
    ident = i == j
    causal = i >= j
    strict = i > j
    d8 = (i // 8) == (j // 8)
    o16 = ((i // 16) == (j // 16)) & ((i // 8) == (j // 8) + 1)
    o32 = ((i // 32) == (j // 32)) & ((i // 16) == (j // 16) + 1)
    o64 = (i // 32) == (j // 32) + 1
    return jnp.asarray(np.stack([ident, causal, strict, d8, o16, o32, o64]).astype(np.float32))


M_IDENT, M_CAUSAL, M_STRICT, M_D8, M_O16, M_O32, M_O64 = range(7)


def _bd_mask(n_cols, col_block):
    r = np.arange(GROUP)[:, None] // CHUNK
    c = (np.arange(n_cols)[None, :] % (4 * col_block)) // col_block
    return jnp.asarray((r == c).astype(np.float32))


def _head_masks(n_heads, width):
    m = np.zeros((SUBLANES, n_heads * width), np.float32)
    for h in range(n_heads):
        m[h, h * width:(h + 1) * width] = 1.0
    return jnp.asarray(m)


def _pad_cols(w):
    return jnp.pad(w, ((0, 0), (0, LANES - w.shape[1])))


def _pad_lanes(v):
    v = v.reshape(1, -1).astype(F32)
    return jnp.pad(v, ((0, 0), (0, LANES - v.shape[1])))


def _project_conv_silu(xb, w_ref, col0, xbuf, cw_ref, cb_ref, dst, tt, width):
    for start in range(0, width, 4 * LANES):
        sl = slice(start, min(start + 4 * LANES, width))
        xbuf[SUBLANES:SUBLANES + tt, sl] = _dot(xb, w_ref[:, col0 + sl.start:col0 + sl.stop])
        acc = cw_ref[0:1, sl] * xbuf[5:5 + tt, sl]
        for k in range(1, CONV_K):
            acc = acc + cw_ref[k:k + 1, sl] * xbuf[5 + k:5 + k + tt, sl]
        if cb_ref is not None:
            acc = acc + cb_ref[:, sl]
        dst[:, sl] = _silu(acc)
    xbuf[0:SUBLANES, :] = xbuf[tt:tt + SUBLANES, :]


def _tile_spec(tt, width, nt):
    return pl.BlockSpec((tt, width), lambda b, i: (b * nt + i, 0))


def _ssd_kernel(h_ref, win_ref, wout_ref, lg_ref, lb_ref,
                cw_ref, cb_ref, alog_ref, dtb_ref, dskip_ref, nw_ref, tri_ref, e_ref, wm_ref, mbd_ref,
                o_ref, zbuf, xbuf, xc, acol, xdt, hst, *, alpha):
    tt = h_ref.shape[0]
    ncx = zbuf.shape[1]
    ncb = xbuf.shape[1] - ncx
    gw = ncx // SSD_GROUPS

    @pl.when(pl.program_id(1) == 0)
    def _():
        xbuf[0:SUBLANES, :] = jnp.zeros((SUBLANES, ncx + ncb), F32)
        hst[...] = jnp.zeros_like(hst)

    hb = h_ref[...].astype(BF16)
    _project_conv_silu(hb, win_ref, ncx, xbuf, cw_ref, cb_ref, xc, tt, ncx + ncb)

    dt = _softplus(_dot(hb, win_ref[:, 2 * ncx + ncb:]) + dtb_ref[...])
    a_cum = _sel_rows(tri_ref, dt * (-jnp.exp(alog_ref[...])))
    for start in range(0, ncx, 4 * LANES):
        sl = slice(start, start + 4 * LANES)
        zbuf[:, sl] = _silu(_dot(hb, win_ref[:, sl]))
        acol[:, sl] = _sel_cols(a_cum, e_ref.at[:, sl])
        xdt[:, sl] = xc[:, sl] * _sel_cols(dt, e_ref.at[0:2 * LANES, sl], 2)

    ident = wm_ref[M_IDENT]
    causal = wm_ref[M_CAUSAL] > 0.5
    mbd = mbd_ref[...].astype(BF16)
    groups = range(SSD_GROUPS)
    halves = range(gw // GROUP)

    def chunk(c, carry):
        r0 = pl.multiple_of(c * CHUNK, CHUNK)
        rows = pl.ds(r0, CHUNK)
        gls = [slice(g * gw, (g + 1) * gw) for g in groups]
        b_gs = [xc[rows, ncx + g * SSD_STATE:ncx + (g + 1) * SSD_STATE].astype(BF16) for g in groups]
        c_gs = [xc[rows, ncx + ncb // 2 + g * SSD_STATE:ncx + ncb // 2 + (g + 1) * SSD_STATE].astype(BF16)
                for g in groups]
        cb_wide = [_dot_nt(c_gs[g], _tile4(b_gs[g])) for g in groups]
        a_gs = [acol[rows, gl] for gl in gls]
        a_lasts = [acol[pl.ds(r0 + CHUNK - 1, 1), gl] for gl in gls]
        x_gs = [xdt[rows, gl] for gl in gls]
        lhs, rhs = [], []
        for g in groups:
            for hh in halves:
                hl = slice(hh * GROUP, (hh + 1) * GROUP)
                a_c = a_gs[g][:, hl]
                a_row = jnp.sum(a_c * ident, axis=0, keepdims=True)
                seg = jnp.exp(jnp.where(causal, a_c - a_row, NEG))
                lhs.append((cb_wide[g] * seg).astype(BF16))
                rhs.append(_block_diag(x_gs[g][:, hl], mbd))
        y_diag = [_dot(a, b) for a, b in zip(lhs, rhs)]
        h_prev = [hst[g] for g in groups]
        y_off = [_dot(c_gs[g], h_prev[g].astype(BF16)) for g in groups]
        upd = [_dot_tn(b_gs[g], (x_gs[g] * jnp.exp(a_lasts[g] - a_gs[g])).astype(BF16)) for g in groups]
        for g in groups:
            gl = gls[g]
            hst[g] = h_prev[g] * jnp.exp(a_lasts[g]) + upd[g]
            y = jnp.concatenate(y_diag[g * len(halves):(g + 1) * len(halves)], axis=1)
            y = y + y_off[g] * jnp.exp(a_gs[g]) + dskip_ref[:, gl] * xc[rows, gl]
            y = y * zbuf[rows, gl]
            xdt[rows, gl] = y * lax.rsqrt(jnp.mean(y * y, -1, keepdims=True) + RMS_EPS) * nw_ref[:, gl]
        return carry

    lax.fori_loop(0, tt // CHUNK, chunk, 0, unroll=True)
    acc = alpha * h_ref[...] + _dot(xdt[...].astype(BF16), wout_ref[...])
    o_ref[...] = _layer_norm(acc, lg_ref[...], lb_ref[...])


def _ssd_layer(h, bsz, s, w_in, conv_w, conv_b, a_log, dt_bias, d_skip, norm_w, w_out, ln_g, ln_b, alpha):
    tt = TT_MIX
    nt = s // tt
    d = h.shape[1]
    n_heads = a_log.shape[0]
    ncx = n_heads * SSD_HEADDIM
    ncb = 2 * SSD_GROUPS * SSD_STATE
    w_in = jnp.concatenate([w_in[:, :2 * ncx + ncb], _pad_cols(w_in[:, 2 * ncx + ncb:])], axis=1).astype(BF16)
    consts = [w_in, w_out.astype(BF16), ln_g.reshape(1, d), ln_b.reshape(1, d),
              conv_w.astype(F32), conv_b.reshape(1, -1).astype(F32), _pad_lanes(a_log), _pad_lanes(dt_bias),
              jnp.repeat(d_skip.astype(F32), SSD_HEADDIM).reshape(1, ncx), norm_w.reshape(1, ncx).astype(F32),
              _chunk_tri(tt, True), _expand_mat(n_heads, SSD_HEADDIM), _wide_masks(), _bd_mask(GROUP, CHUNK)]
    return pl.pallas_call(
        functools.partial(_ssd_kernel, alpha=alpha),
        grid=(bsz, nt),
        in_specs=[_tile_spec(tt, d, nt)] + [_const_spec(c.shape, 2) for c in consts],
        out_specs=_tile_spec(tt, d, nt),
        out_shape=jax.ShapeDtypeStruct((bsz * s, d), F32),
        scratch_shapes=[pltpu.VMEM((tt, ncx), F32),
                        pltpu.VMEM((tt + SUBLANES, ncx + ncb), F32), pltpu.VMEM((tt, ncx + ncb), F32),
                        pltpu.VMEM((tt, ncx), F32), pltpu.VMEM((tt, ncx), F32),
                        pltpu.VMEM((SSD_GROUPS, SSD_STATE, ncx // SSD_GROUPS), F32)],
        compiler_params=_cparams("parallel", "arbitrary"),
        name="ssd_layer",
    )(h, *consts)


def _per_head(x, hm_in, hm_outs, reduce_fn, fill):
    outs = [0.0] * len(hm_outs)
    for h in range(MLSTM_HEADS):
        r = reduce_fn(jnp.where(hm_in[h:h + 1, :] > 0.5, x, fill), axis=-1, keepdims=True)
        outs = [o + r * hm[h:h + 1, :] for o, hm in zip(outs, hm_outs)]
    return outs


def _mlstm_kernel(h_ref, oa_ref, win_ref, woa_ref, wob_ref, lg_ref, lb_ref, ib_ref, fb_ref,
                  tri_ref, e2_ref, e5_ref, wm_ref, mbd_ref, mbdv_ref, hm2_ref, hm5_ref,
                  o_ref, qkvo, ob, b2, li2, b5, li5, cst, nst, m5s, m2s, *, alpha):
    tt = h_ref.shape[0]
    kw = MLSTM_HEADS * MLSTM_DK
    vw = MLSTM_HEADS * MLSTM_DV
    scale = MLSTM_DK ** -0.5

    @pl.when(pl.program_id(1) == 0)
    def _():
        cst[...] = jnp.zeros_like(cst)
        nst[...] = jnp.zeros_like(nst)
        m5s[...] = jnp.zeros_like(m5s)
        m2s[...] = jnp.zeros_like(m2s)

    hb = h_ref[...].astype(BF16)
    gates = _dot(hb, win_ref[:, 2 * kw + 2 * vw:])
    log_i = gates[:, 0:LANES] + ib_ref[...]
    log_f = -_softplus(-(gates[:, LANES:] + fb_ref[...]))
    for start in range(0, 2 * kw + 2 * vw, 4 * LANES):
        sl = slice(start, start + 4 * LANES)
        qkvo[:, sl] = _dot(hb, win_ref[:, sl])

    b = _sel_rows(tri_ref, log_f)
    for src, d2, d5 in ((b, b2, b5), (log_i, li2, li5)):
        d2[...] = _sel_cols(src, e2_ref)
        d5[...] = _sel_cols(src, e5_ref)

    ident = wm_ref[M_IDENT]
    causal = wm_ref[M_CAUSAL] > 0.5
    mbd = mbd_ref[...].astype(BF16)
    mbdv_f = mbdv_ref[...]
    mbdv = mbdv_f.astype(BF16)
    hm2 = hm2_ref[...]
    hm5 = hm5_ref[...]
    chunks = [slice(c * CHUNK, (c + 1) * CHUNK) for c in range(tt // CHUNK)]

    qs = [qkvo[rows, 0:kw] for rows in chunks]
    ks = [qkvo[rows, kw:2 * kw] * scale for rows in chunks]
    vs = [qkvo[rows, 2 * kw:2 * kw + vw] for rows in chunks]
    d_logs, mi2s, mi5s, me2s, me5s, k_ends = [], [], [], [], [], []
    for rows, k in zip(chunks, ks):
        bc2, lic2 = b2[rows, :], li2[rows, :]
        row_v = jnp.sum((lic2 - bc2) * ident, axis=0, keepdims=True)
        d_log = jnp.where(causal, bc2 + row_v, NEG)
        mi2, mi5 = _per_head(d_log, hm2, (hm2, hm5), jnp.max, NEG)
        d_logs.append(d_log)
        mi2s.append(mi2)
        mi5s.append(mi5)
        bl2, bl5 = b2[rows.stop - 1:rows.stop, :], b5[rows.stop - 1:rows.stop, :]
        w2 = bl2 - bc2 + lic2
        me2 = jnp.max(w2, axis=0, keepdims=True)
        me2s.append(me2)
        me5s.append(jnp.max(bl5 - b5[rows, :] + li5[rows, :], axis=0, keepdims=True))
        k_ends.append(k * jnp.exp(w2 - me2))
    s_qks = [_dot_nt(q.astype(BF16), _block_diag(k, mbd)) * jnp.exp(d_log - mi2)
             for q, k, d_log, mi2 in zip(qs, ks, d_logs, mi2s)]
    c_ends = [_dot_tn(k_end.astype(BF16), v.astype(BF16)) * mbdv_f for k_end, v in zip(k_ends, vs)]
    num_is = [_dot(s_qk.astype(BF16), _block_diag(v, mbdv)) for s_qk, v in zip(s_qks, vs)]
    den_is = [_per_head(s_qk, hm2, (hm5,), jnp.sum, 0.0)[0] for s_qk in s_qks]

    c_st, n_st, m5, m2 = cst[...], nst[...], m5s[...], m2s[...]
    c_sts, n_sts, m5l = [], [], []
    for rows, c_end, k_end, me5, me2 in zip(chunks, c_ends, k_ends, me5s, me2s):
        c_sts.append(c_st)
        n_sts.append(n_st)
        m5l.append(m5)
        bl5, bl2 = b5[rows.stop - 1:rows.stop, :], b2[rows.stop - 1:rows.stop, :]
        mn5 = jnp.maximum(bl5 + m5, me5)
        mn2 = jnp.maximum(bl2 + m2, me2)
        c_st = c_st * jnp.exp(bl5 + m5 - mn5) + c_end * jnp.exp(me5 - mn5)
        n_st = n_st * jnp.exp(bl2 + m2 - mn2) + jnp.sum(k_end, axis=0, keepdims=True) * jnp.exp(me2 - mn2)
        m5, m2 = mn5, mn2
    cst[...], nst[...], m5s[...], m2s[...] = c_st, n_st, m5, m2
    q_cs = [_dot(q.astype(BF16), c.astype(BF16)) for q, c in zip(qs, c_sts)]
    for rows, q, q_c, n, m, mi5, num_i, den_i in zip(chunks, qs, q_cs, n_sts, m5l, mi5s, num_is, den_is):
        a5 = b5[rows, :] + m
        m_t = jnp.maximum(a5, mi5)
        ea, ei = jnp.exp(a5 - m_t), jnp.exp(mi5 - m_t)
        (qn,) = _per_head(q * n, hm2, (hm5,), jnp.sum, 0.0)
        h = (ea * q_c + ei * num_i) / jnp.maximum(jnp.abs(ea * qn + ei * den_i), jnp.exp(-m_t))
        ob[rows, :] = _sigmoid(qkvo[rows, 2 * kw + vw:]) * h

    acc = (alpha * h_ref[...] + _dot(oa_ref[...].astype(BF16), woa_ref[...])
           + _dot(ob[...].astype(BF16), wob_ref[...]))
    o_ref[...] = _layer_norm(acc, lg_ref[...], lb_ref[...])


def _mlstm_layer(h, o_a, bsz, s, w_in, i_bias, f_bias, w_out_a, w_out_b, ln_g, ln_b, alpha):
    tt = TT_MIX
    nt = s // tt
    d = h.shape[1]
    kw = MLSTM_HEADS * MLSTM_DK
    vw = MLSTM_HEADS * MLSTM_DV
    main = 2 * kw + 2 * vw
    w_in = jnp.concatenate([w_in[:, :main], _pad_cols(w_in[:, main:main + MLSTM_HEADS]),
                            _pad_cols(w_in[:, main + MLSTM_HEADS:])], axis=1).astype(BF16)
    consts = [w_in, w_out_a.astype(BF16), w_out_b.astype(BF16), ln_g.reshape(1, d), ln_b.reshape(1, d),
              _pad_lanes(i_bias), _pad_lanes(f_bias), _chunk_tri(tt, True),
              _expand_mat(MLSTM_HEADS, MLSTM_DK), _expand_mat(MLSTM_HEADS, MLSTM_DV), _wide_masks(),
              _bd_mask(kw, MLSTM_DK), _bd_mask(vw, MLSTM_DV),
              _head_masks(MLSTM_HEADS, MLSTM_DK), _head_masks(MLSTM_HEADS, MLSTM_DV)]
    return pl.pallas_call(
        functools.partial(_mlstm_kernel, alpha=alpha),
        grid=(bsz, nt),
        in_specs=[_tile_spec(tt, d, nt), _tile_spec(tt, o_a.shape[1], nt)]
                 + [_const_spec(c.shape, 2) for c in consts],
        out_specs=_tile_spec(tt, d, nt),
        out_shape=jax.ShapeDtypeStruct((bsz * s, d), F32),
        scratch_shapes=[pltpu.VMEM((tt, main), F32), pltpu.VMEM((tt, vw), F32)]
                       + [pltpu.VMEM((tt, kw), F32)] * 2 + [pltpu.VMEM((tt, vw), F32)] * 2
                       + [pltpu.VMEM((kw, vw), F32), pltpu.VMEM((1, kw), F32),
                          pltpu.VMEM((1, vw), F32), pltpu.VMEM((1, kw), F32)],
        compiler_params=_cparams("parallel", "arbitrary"),
        name="mlstm_layer",
    )(h, o_a, *consts)


def _unit_lower_inverse(ls, wm_ref, mbd):
    def mm(xs, ys):
        return [_dot(a.astype(BF16), _block_diag(b, mbd)) for a, b in zip(xs, ys)]

    ident = wm_ref[M_IDENT]
    d8 = wm_ref[M_D8]
    l8 = [l * d8 for l in ls]
    l8_2 = mm(l8, l8)
    l8_4 = mm(l8_2, l8_2)
    t = mm(mm([ident - a for a in l8], [ident + a for a in l8_2]), [ident + a for a in l8_4])
    for level in (M_O16, M_O32, M_O64):
        off = wm_ref[level]
        corr = mm(mm(t, [l * off for l in ls]), t)
        t = [a - b for a, b in zip(t, corr)]
    return t


def _gdn_kernel(h_ref, win_ref, cw_ref, alog_ref, dtb_ref, nw_ref,
                tri_ref, e_ref, seg_ref, wm_ref, mbd_ref, mbd2_ref,
                o_ref, zbuf, xbuf, xc, gcw, btw, u0b, wdb, atb, qdb, kdb, ob, sst):
    tt = h_ref.shape[0]
    hw = GDN_HEADS * GDN_DK
    n_groups = hw // GROUP

    @pl.when(pl.program_id(1) == 0)
    def _():
        xbuf[0:SUBLANES, :] = jnp.zeros((SUBLANES, 3 * hw), F32)
        sst[...] = jnp.zeros_like(sst)

    hb = h_ref[...].astype(BF16)
    _project_conv_silu(hb, win_ref, 0, xbuf, cw_ref, None, xc, tt, 3 * hw)
    gates = _dot(hb, win_ref[:, 4 * hw:])
    beta = _sigmoid(gates[:, 0:LANES])
    g = -jnp.exp(alog_ref[...]) * _softplus(gates[:, LANES:] + dtb_ref[...])
    zbuf[...] = _dot(hb, win_ref[:, 3 * hw:4 * hw])

    for j, s in ((0, GDN_DK ** -0.5), (1, 1.0)):
        sl = slice(j * hw, (j + 1) * hw)
        x = xc[:, sl]
        xc[:, sl] = x * (lax.rsqrt(_sel_cols(x * x, seg_ref, 2) + L2_EPS) * s)

    gcw[...] = _sel_cols(_sel_rows(tri_ref, g), e_ref)
    btw[...] = _sel_cols(beta, e_ref)

    ident = wm_ref[M_IDENT]
    causal = wm_ref[M_CAUSAL] > 0.5
    strict = wm_ref[M_STRICT]
    mbd_f = mbd_ref[...]
    mbd = mbd_f.astype(BF16)
    mbd2 = mbd2_ref[...].astype(BF16)
    groups = range(n_groups)
    lanes = [slice(grp * GROUP, (grp + 1) * GROUP) for grp in groups]

    tiles = [(slice(c * CHUNK, (c + 1) * CHUNK), grp) for c in range(tt // CHUNK) for grp in groups]
    kbs, kbds, decs, lowers = [], [], [], []
    for rows, grp in tiles:
        l = lanes[grp]
        gc = gcw[rows, l]
        k = xc[rows, hw + grp * GROUP:hw + (grp + 1) * GROUP]
        g_row = jnp.sum(gc * ident, axis=0, keepdims=True)
        decs.append(jnp.exp(jnp.where(causal, gc - g_row, NEG)))
        kbs.append(k * btw[rows, l])
        kbds.append(_block_diag(k, mbd))
        kdb[rows, l] = k * jnp.exp(gc[CHUNK - 1:CHUNK, :] - gc)
    for kb, kbd, dec in zip(kbs, kbds, decs):
        lowers.append(_dot_nt(kb.astype(BF16), kbd) * dec * strict)
    for (rows, grp), kbd, dec in zip(tiles, kbds, decs):
        q = xc[rows, lanes[grp]]
        atb[rows, lanes[grp]] = _dot_nt(q.astype(BF16), kbd) * dec
    t_invs = _unit_lower_inverse(lowers, wm_ref, mbd)
    rbds = []
    for (rows, grp), kb in zip(tiles, kbs):
        l = lanes[grp]
        eg = jnp.exp(gcw[rows, l])
        v = xc[rows, 2 * hw + grp * GROUP:2 * hw + (grp + 1) * GROUP]
        rbds.append(_block_diag(jnp.concatenate([v * btw[rows, l], kb * eg], axis=1), mbd2))
        qdb[rows, l] = xc[rows, l] * eg
    for (rows, grp), t_inv, rbd in zip(tiles, t_invs, rbds):
        sol = _dot(t_inv.astype(BF16), rbd)
        u0b[rows, lanes[grp]] = sol[:, :GROUP]
        wdb[rows, lanes[grp]] = sol[:, GROUP:]

    def chunk(c, carry):
        r0 = pl.multiple_of(c * CHUNK, CHUNK)
        rows = pl.ds(r0, CHUNK)
        sts = [sst[grp] for grp in groups]
        rs = [_dot(jnp.concatenate([wdb[rows, l], qdb[rows, l]], axis=0).astype(BF16), st.astype(BF16))
              for l, st in zip(lanes, sts)]
        us = [u0b[rows, l] - r[:CHUNK] for l, r in zip(lanes, rs)]
        intra = [_dot(atb[rows, l].astype(BF16), _block_diag(u, mbd)) for l, u in zip(lanes, us)]
        upd = [_dot_tn(kdb[rows, l].astype(BF16), u.astype(BF16)) for l, u in zip(lanes, us)]
        for grp in groups:
            l = lanes[grp]
            ob[rows, l] = rs[grp][CHUNK:] + intra[grp]
            g_last = gcw[pl.ds(r0 + CHUNK - 1, 1), l]
            sst[grp] = sts[grp] * jnp.exp(g_last) + upd[grp] * mbd_f
        return carry

    lax.fori_loop(0, tt // CHUNK, chunk, 0, unroll=True)

    o = ob[...]
    ms = _sel_cols(o * o, seg_ref, 2) * (1.0 / GDN_DV)
    o_ref[...] = o * lax.rsqrt(ms + RMS_EPS) * nw_ref[...] * _silu(zbuf[...])


def _gdn_mixer(h, bsz, s, w_in, conv_w, a_log, dt_bias, norm_w):
    tt = TT_MIX
    nt = s // tt
    d = h.shape[1]
    hw = GDN_HEADS * GDN_DK
    w_in = jnp.concatenate([w_in[:, :4 * hw], _pad_cols(w_in[:, 4 * hw:4 * hw + GDN_HEADS]),
                            _pad_cols(w_in[:, 4 * hw + GDN_HEADS:])], axis=1).astype(BF16)
    consts = [w_in, conv_w.astype(F32), _pad_lanes(a_log), _pad_lanes(dt_bias),
              jnp.tile(norm_w.astype(F32), GDN_HEADS).reshape(1, hw),
              _chunk_tri(tt, True), _expand_mat(GDN_HEADS, GDN_DK), _segment_ones(hw, GDN_DK), _wide_masks(),
              _bd_mask(GROUP, CHUNK), _bd_mask(2 * GROUP, CHUNK)]
    return pl.pallas_call(
        _gdn_kernel,
        grid=(bsz, nt),
        in_specs=[_tile_spec(tt, d, nt)] + [_const_spec(c.shape, 2) for c in consts],
        out_specs=_tile_spec(tt, hw, nt),
        out_shape=jax.ShapeDtypeStruct((bsz * s, hw), F32),
        scratch_shapes=[pltpu.VMEM((tt, hw), F32),
                        pltpu.VMEM((tt + SUBLANES, 3 * hw), F32), pltpu.VMEM((tt, 3 * hw), F32)]
                       + [pltpu.VMEM((tt, hw), F32)] * 8
                       + [pltpu.VMEM((hw // GROUP, GROUP, GROUP), F32)],
        compiler_params=_cparams("parallel", "arbitrary"),
        name="gdn_mixer",
    )(h, *consts)


def kernel(x, ffn_pre_w_gate, ffn_pre_w_up, ffn_pre_w_down, ln_pre_g, ln_pre_b, hyb_w_in, hyb_conv_w, gdn_a_log, gdn_dt_bias, gdn_norm_w, mlstm_i_bias, mlstm_f_bias, hyb_w_out, ssd_w_in, ssd_conv_w, ssd_conv_b, ssd_a_log, ssd_dt_bias, ssd_d_skip, ssd_norm_w, ssd_w_out, ln_mix_g, ln_mix_b, ffn_post_w_gate, ffn_post_w_up, ffn_post_w_down, ln_post_g, ln_post_b):
    bsz, s, d = x.shape
    depth = ffn_pre_w_gate.shape[0]
    alpha = (2 * depth) ** 0.25
    gdn_in = 4 * GDN_HEADS * GDN_DK + 2 * GDN_HEADS
    gdn_out = GDN_HEADS * GDN_DV
    pre = [w.astype(BF16) for w in (ffn_pre_w_gate, ffn_pre_w_up, ffn_pre_w_down)]
    post = [w.astype(BF16) for w in (ffn_post_w_gate, ffn_post_w_up, ffn_post_w_down)]
    h = x.reshape(bsz * s, d)
    for l in range(depth):
        h = _ffn(h, *pre, ln_pre_g, ln_pre_b, l, alpha)
        j = l // 2
        if l % 2 == 0:
            o_a = _gdn_mixer(h, bsz, s, hyb_w_in[j][:, :gdn_in], hyb_conv_w[j], gdn_a_log[j], gdn_dt_bias[j],
                             gdn_norm_w[j])
            h = _mlstm_layer(h, o_a, bsz, s, hyb_w_in[j][:, gdn_in:], mlstm_i_bias[j], mlstm_f_bias[j],
                             hyb_w_out[j][:gdn_out], hyb_w_out[j][gdn_out:], ln_mix_g[l], ln_mix_b[l], alpha)
        else:
            h = _ssd_layer(h, bsz, s, ssd_w_in[j], ssd_conv_w[j], ssd_conv_b[j], ssd_a_log[j], ssd_dt_bias[j],
                           ssd_d_skip[j], ssd_norm_w[j], ssd_w_out[j], ln_mix_g[l], ln_mix_b[l], alpha)
        h = _ffn(h, *post, ln_post_g, ln_post_b, l, alpha)
    return h.reshape(bsz, s, d)
```

```python
import functools

import jax
import jax.numpy as jnp
import numpy as np
from jax import lax
from jax.experimental import pallas as pl
from jax.experimental.pallas import tpu as pltpu

F32 = jnp.float32
BF16 = jnp.bfloat16

LN_EPS = 1e-5
RMS_EPS = 1e-6
L2_EPS = 1e-6
CHUNK = 64
CONV_K = 4
NEG = -1e30

GDN_HEADS, GDN_DK, GDN_DV = 8, 64, 64
MLSTM_HEADS, MLSTM_DK, MLSTM_DV = 4, 64, 128
SSD_HEADDIM, SSD_GROUPS, SSD_STATE = 64, 4, 128

LANES = 128
SUBLANES = 8
VMEM_LIMIT = 56 * 1024 * 1024
GROUP = 4 * CHUNK

TM_DENSE = 512
TT_MIX = 256
TT_WIDE = 512


def _cparams(*sem):
    return pltpu.CompilerParams(dimension_semantics=sem, vmem_limit_bytes=VMEM_LIMIT)


def _dot(a, b):
    return jnp.dot(a, b, preferred_element_type=F32)


def _dot_nt(a, b):
    return lax.dot_general(a, b, (((1,), (1,)), ((), ())), preferred_element_type=F32)


def _dot_tn(a, b):
    return lax.dot_general(a, b, (((0,), (0,)), ((), ())), preferred_element_type=F32)


def _layer_norm(y, g, b):
    mu = jnp.mean(y, -1, keepdims=True)
    yc = y - mu
    var = jnp.mean(yc * yc, -1, keepdims=True)
    return yc * lax.rsqrt(var + LN_EPS) * g + b


def _sigmoid(x):
    return 1.0 / (1.0 + jnp.exp(-x))


def _silu(x):
    return x * _sigmoid(x)


def _softplus(x):
    return jnp.maximum(x, 0.0) + jnp.log(1.0 + jnp.exp(-jnp.abs(x)))


def _split(x, parts):
    out = []
    r = x
    for p in range(parts):
        h = r.astype(BF16)
        out.append(h)
        if p + 1 < parts:
            r = r - h.astype(F32)
    return out


def _sel_rows(m01x3_ref, x):
    rows = m01x3_ref.shape[0]
    return jnp.concatenate([_dot(m01x3_ref[...], jnp.concatenate(_split(x[r:r + rows], 3), axis=0))
                            for r in range(0, x.shape[0], rows)], axis=0)


def _sel_cols(x, m01x3_ref, parts=3):
    return _dot(jnp.concatenate(_split(x, parts), axis=1), m01x3_ref[...])


def _head_sums(x, head_width):
    low = lax.broadcasted_iota(jnp.int32, (1, LANES), 1) < head_width
    out = []
    for start in range(0, x.shape[1], LANES):
        blk = x[:, start:start + LANES]
        s_low = jnp.sum(jnp.where(low, blk, 0.0), axis=-1, keepdims=True)
        s_high = jnp.sum(jnp.where(low, 0.0, blk), axis=-1, keepdims=True)
        out.append(jnp.where(low, s_low, s_high))
    return jnp.concatenate(out, axis=1)


def _tile4(x):
    return jnp.concatenate([x, x, x, x], axis=0)


def _block_diag(x, mask_bf16):
    return _tile4(x.astype(BF16)) * mask_bf16


def _ffn_kernel(x_ref, wg_ref, wu_ref, wd_ref, g_ref, b_ref, o_ref, *, fc, alpha):
    x = x_ref[...]
    xb = x.astype(BF16)
    acc = jnp.zeros_like(x)
    for c in range(wg_ref.shape[1] // fc):
        sl = slice(c * fc, (c + 1) * fc)
        hg = _dot(xb, wg_ref[:, sl])
        hu = _dot(xb, wu_ref[:, sl])
        acc = acc + _dot((_silu(hg) * hu).astype(BF16), wd_ref[sl, :])
    o_ref[...] = _layer_norm(alpha * x + 0.5 * acc, g_ref[...], b_ref[...])


def _const_spec(shape, grid_rank=1):
    zeros = (0,) * len(shape)
    if grid_rank == 1:
        return pl.BlockSpec(shape, lambda i: zeros, pipeline_mode=pl.Buffered(1))
    return pl.BlockSpec(shape, lambda b, i: zeros, pipeline_mode=pl.Buffered(1))


def _ffn(x, wg, wu, wd, g, b, layer, alpha):
    t, d = x.shape
    f = wg.shape[2]
    tm = TM_DENSE

    def layer_spec(rows, cols):
        return pl.BlockSpec((None, rows, cols), lambda i: (layer, 0, 0), pipeline_mode=pl.Buffered(1))

    return pl.pallas_call(
        functools.partial(_ffn_kernel, fc=2 * LANES, alpha=alpha),
        grid=(t // tm,),
        in_specs=[pl.BlockSpec((tm, d), lambda i: (i, 0)),
                  layer_spec(d, f), layer_spec(d, f), layer_spec(f, d),
                  _const_spec((1, d)), _const_spec((1, d))],
        out_specs=pl.BlockSpec((tm, d), lambda i: (i, 0)),
        out_shape=jax.ShapeDtypeStruct((t, d), F32),
        compiler_params=_cparams("parallel"),
        name="ffn",
    )(x, wg, wu, wd, g[layer].reshape(1, d), b[layer].reshape(1, d))


def _chunk_tri(tt, inclusive_lower):
    i = np.arange(tt)[:, None]
    j = np.arange(tt)[None, :]
    same = (i // CHUNK) == (j // CHUNK)
    m = same & (j <= i) if inclusive_lower else same
    return jnp.asarray(np.tile(m.astype(np.float32), (1, 3)), BF16)


def _expand_mat(n_heads, width, parts=3):
    e = np.zeros((LANES, n_heads * width), np.float32)
    for h in range(n_heads):
        e[h, h * width:(h + 1) * width] = 1.0
    return jnp.asarray(np.tile(e, (parts, 1)), BF16)


def _wide_masks():
    i = np.arange(CHUNK)[:, None]
    j = (np.arange(GROUP) % CHUNK)[None, :]
    ident = i == j
    causal = i >= j
    strict = i > j
    d8 = (i // 8) == (j // 8)
    o16 = ((i // 16) == (j // 16)) & ((i // 8) == (j // 8) + 1)
    o32 = ((i // 32) == (j // 32)) & ((i // 16) == (j // 16) + 1)
    o64 = (i // 32) == (j // 32) + 1
    return jnp.asarray(np.stack([ident, causal, strict, d8, o16, o32, o64]).astype(np.float32))


M_IDENT, M_CAUSAL, M_STRICT, M_D8, M_O16, M_O32, M_O64 = range(7)


def _bd_mask(n_cols, col_block):
    r = np.arange(GROUP)[:, None] // CHUNK
    c = (np.arange(n_cols)[None, :] % (4 * col_block)) // col_block
    return jnp.asarray((r == c).astype(np.float32))


def _head_masks(n_heads, width):
    m = np.zeros((SUBLANES, n_heads * width), np.float32)
    for h in range(n_heads):
        m[h, h * width:(h + 1) * width] = 1.0
    return jnp.asarray(m)


def _pad_cols(w):
    return jnp.pad(w, ((0, 0), (0, LANES - w.shape[1])))


def _pad_lanes(v):
    v = v.reshape(1, -1).astype(F32)
    return jnp.pad(v, ((0, 0), (0, LANES - v.shape[1])))


def _project_conv_silu(xb, w_ref, col0, xbuf, cw_ref, cb_ref, dst, tt, width):
    for start in range(0, width, 4 * LANES):
        sl = slice(start, min(start + 4 * LANES, width))
        xbuf[SUBLANES:SUBLANES + tt, sl] = _dot(xb, w_ref[:, col0 + sl.start:col0 + sl.stop])
        acc = cw_ref[0:1, sl] * xbuf[5:5 + tt, sl]
        for k in range(1, CONV_K):
            acc = acc + cw_ref[k:k + 1, sl] * xbuf[5 + k:5 + k + tt, sl]
        if cb_ref is not None:
            acc = acc + cb_ref[:, sl]
        dst[:, sl] = _silu(acc)
    xbuf[0:SUBLANES, :] = xbuf[tt:tt + SUBLANES, :]


def _tile_spec(tt, width, nt):
    return pl.BlockSpec((tt, width), lambda b, i: (b * nt + i, 0))


def _ssd_kernel(h_ref, win_ref, wout_ref, lg_ref, lb_ref,
                cw_ref, cb_ref, alog_ref, dtb_ref, dskip_ref, nw_ref, tri_ref, e_ref, wm_ref, mbd_ref,
                o_ref, zbuf, xbuf, xc, acol, xdt, hst, *, alpha):
    tt = h_ref.shape[0]
    ncx = zbuf.shape[1]
    ncb = xbuf.shape[1] - ncx
    gw = ncx // SSD_GROUPS

    @pl.when(pl.program_id(1) == 0)
    def _():
        xbuf[0:SUBLANES, :] = jnp.zeros((SUBLANES, ncx + ncb), F32)
        hst[...] = jnp.zeros_like(hst)

    hb = h_ref[...].astype(BF16)
    _project_conv_silu(hb, win_ref, ncx, xbuf, cw_ref, cb_ref, xc, tt, ncx + ncb)

    dt = _softplus(_dot(hb, win_ref[:, 2 * ncx + ncb:]) + dtb_ref[...])
    a_cum = _sel_rows(tri_ref, dt * (-jnp.exp(alog_ref[...])))
    for start in range(0, ncx, 4 * LANES):
        sl = slice(start, start + 4 * LANES)
        zbuf[:, sl] = _silu(_dot(hb, win_ref[:, sl]))
        acol[:, sl] = _sel_cols(a_cum, e_ref.at[:, sl])
        xdt[:, sl] = xc[:, sl] * _sel_cols(dt, e_ref.at[0:2 * LANES, sl], 2)

    ident = wm_ref[M_IDENT]
    causal = wm_ref[M_CAUSAL] > 0.5
    mbd = mbd_ref[...].astype(BF16)
    groups = range(SSD_GROUPS)
    halves = range(gw // GROUP)

    def chunk(c, carry):
        r0 = pl.multiple_of(c * CHUNK, CHUNK)
        rows = pl.ds(r0, CHUNK)
        gls = [slice(g * gw, (g + 1) * gw) for g in groups]
        b_gs = [xc[rows, ncx + g * SSD_STATE:ncx + (g + 1) * SSD_STATE].astype(BF16) for g in groups]
        c_gs = [xc[rows, ncx + ncb // 2 + g * SSD_STATE:ncx + ncb // 2 + (g + 1) * SSD_STATE].astype(BF16)
                for g in groups]
        cb_wide = [_dot_nt(c_gs[g], _tile4(b_gs[g])) for g in groups]
        a_gs = [acol[rows, gl] for gl in gls]
        a_lasts = [acol[pl.ds(r0 + CHUNK - 1, 1), gl] for gl in gls]
        x_gs = [xdt[rows, gl] for gl in gls]
        lhs, rhs = [], []
        for g in groups:
            for hh in halves:
                hl = slice(hh * GROUP, (hh + 1) * GROUP)
                a_c = a_gs[g][:, hl]
                a_row = jnp.sum(a_c * ident, axis=0, keepdims=True)
                seg = jnp.exp(jnp.where(causal, a_c - a_row, NEG))
                lhs.append((cb_wide[g] * seg).astype(BF16))
                rhs.append(_block_diag(x_gs[g][:, hl], mbd))
        y_diag = [_dot(a, b) for a, b in zip(lhs, rhs)]
        h_prev = [hst[g] for g in groups]
        y_off = [_dot(c_gs[g], h_prev[g].astype(BF16)) for g in groups]
        upd = [_dot_tn(b_gs[g], (x_gs[g] * jnp.exp(a_lasts[g] - a_gs[g])).astype(BF16)) for g in groups]
        for g in groups:
            gl = gls[g]
            hst[g] = h_prev[g] * jnp.exp(a_lasts[g]) + upd[g]
            y = jnp.concatenate(y_diag[g * len(halves):(g + 1) * len(halves)], axis=1)
            y = y + y_off[g] * jnp.exp(a_gs[g]) + dskip_ref[:, gl] * xc[rows, gl]
            y = y * zbuf[rows, gl]
            xdt[rows, gl] = y * lax.rsqrt(jnp.mean(y * y, -1, keepdims=True) + RMS_EPS) * nw_ref[:, gl]
        return carry

    lax.fori_loop(0, tt // CHUNK, chunk, 0, unroll=True)
    acc = alpha * h_ref[...] + _dot(xdt[...].astype(BF16), wout_ref[...])
    o_ref[...] = _layer_norm(acc, lg_ref[...], lb_ref[...])


def _ssd_layer(h, bsz, s, w_in, conv_w, conv_b, a_log, dt_bias, d_skip, norm_w, w_out, ln_g, ln_b, alpha):
    tt = TT_MIX
    nt = s // tt
    d = h.shape[1]
    n_heads = a_log.shape[0]
    ncx = n_heads * SSD_HEADDIM
    ncb = 2 * SSD_GROUPS * SSD_STATE
    w_in = jnp.concatenate([w_in[:, :2 * ncx + ncb], _pad_cols(w_in[:, 2 * ncx + ncb:])], axis=1).astype(BF16)
    consts = [w_in, w_out.astype(BF16), ln_g.reshape(1, d), ln_b.reshape(1, d),
              conv_w.astype(F32), conv_b.reshape(1, -1).astype(F32), _pad_lanes(a_log), _pad_lanes(dt_bias),
              jnp.repeat(d_skip.astype(F32), SSD_HEADDIM).reshape(1, ncx), norm_w.reshape(1, ncx).astype(F32),
              _chunk_tri(tt, True), _expand_mat(n_heads, SSD_HEADDIM), _wide_masks(), _bd_mask(GROUP, CHUNK)]
    return pl.pallas_call(
        functools.partial(_ssd_kernel, alpha=alpha),
        grid=(bsz, nt),
        in_specs=[_tile_spec(tt, d, nt)] + [_const_spec(c.shape, 2) for c in consts],
        out_specs=_tile_spec(tt, d, nt),
        out_shape=jax.ShapeDtypeStruct((bsz * s, d), F32),
        scratch_shapes=[pltpu.VMEM((tt, ncx), F32),
                        pltpu.VMEM((tt + SUBLANES, ncx + ncb), F32), pltpu.VMEM((tt, ncx + ncb), F32),
                        pltpu.VMEM((tt, ncx), F32), pltpu.VMEM((tt, ncx), F32),
                        pltpu.VMEM((SSD_GROUPS, SSD_STATE, ncx // SSD_GROUPS), F32)],
        compiler_params=_cparams("parallel", "arbitrary"),
        name="ssd_layer",
    )(h, *consts)


def _per_head(x, hm_in, hm_outs, reduce_fn, fill):
    outs = [None] * len(hm_outs)
    for h in range(MLSTM_HEADS):
        r = reduce_fn(jnp.where(hm_in[h:h + 1, :] > 0.5, x, fill), axis=-1, keepdims=True)
        outs = [jnp.broadcast_to(r, (x.shape[0], hm.shape[1])) if o is None
                else jnp.where(hm[h:h + 1, :] > 0.5, r, o) for o, hm in zip(outs, hm_outs)]
    return outs


def _mlstm_kernel(h_ref, oa_ref, win_ref, woa_ref, wob_ref, lg_ref, lb_ref, ib_ref, fb_ref,
                  tri_ref, e2_ref, e5_ref, wm_ref, mbd_ref, mbdv_ref, hm2_ref, hm5_ref,
                  o_ref, qkvo, ob, b2, li2, b5, li5, cst, nst, m5s, m2s, *, alpha):
    tt = h_ref.shape[0]
    kw = MLSTM_HEADS * MLSTM_DK
    vw = MLSTM_HEADS * MLSTM_DV
    scale = MLSTM_DK ** -0.5

    @pl.when(pl.program_id(1) == 0)
    def _():
        cst[...] = jnp.zeros_like(cst)
        nst[...] = jnp.zeros_like(nst)
        m5s[...] = jnp.zeros_like(m5s)
        m2s[...] = jnp.zeros_like(m2s)

    hb = h_ref[...].astype(BF16)
    gates = _dot(hb, win_ref[:, 2 * kw + 2 * vw:])
    log_i = gates[:, 0:LANES] + ib_ref[...]
    log_f = -_softplus(-(gates[:, LANES:] + fb_ref[...]))
    for start in range(0, 2 * kw + 2 * vw, 4 * LANES):
        sl = slice(start, start + 4 * LANES)
        qkvo[:, sl] = _dot(hb, win_ref[:, sl])

    b = _sel_rows(tri_ref, log_f)
    for src, d2, d5 in ((b, b2, b5), (log_i, li2, li5)):
        d2[...] = _sel_cols(src, e2_ref)
        d5[...] = _sel_cols(src, e5_ref)

    ident = wm_ref[M_IDENT]
    causal = wm_ref[M_CAUSAL] > 0.5
    mbd = mbd_ref[...].astype(BF16)
    mbdv_f = mbdv_ref[...]
    mbdv = mbdv_f.astype(BF16)
    hm2 = hm2_ref[...]
    hm5 = hm5_ref[...]
    chunks = [slice(c * CHUNK, (c + 1) * CHUNK) for c in range(tt // CHUNK)]

    qs = [qkvo[rows, 0:kw] for rows in chunks]
    ks = [qkvo[rows, kw:2 * kw] * scale for rows in chunks]
    vs = [qkvo[rows, 2 * kw:2 * kw + vw] for rows in chunks]
    d_logs, mi2s, mi5s, me2s, me5s, k_ends = [], [], [], [], [], []
    for rows, k in zip(chunks, ks):
        bc2, lic2 = b2[rows, :], li2[rows, :]
        row_v = jnp.sum((lic2 - bc2) * ident, axis=0, keepdims=True)
        d_log = jnp.where(causal, bc2 + row_v, NEG)
        mi2, mi5 = _per_head(d_log, hm2, (hm2, hm5), jnp.max, NEG)
        d_logs.append(d_log)
        mi2s.append(mi2)
        mi5s.append(mi5)
        bl2, bl5 = b2[rows.stop - 1:rows.stop, :], b5[rows.stop - 1:rows.stop, :]
        w2 = bl2 - bc2 + lic2
        me2 = jnp.max(w2, axis=0, keepdims=True)
        me2s.append(me2)
        me5s.append(jnp.max(bl5 - b5[rows, :] + li5[rows, :], axis=0, keepdims=True))
        k_ends.append(k * jnp.exp(w2 - me2))
    s_qks = [_dot_nt(q.astype(BF16), _block_diag(k, mbd)) * jnp.exp(d_log - mi2)
             for q, k, d_log, mi2 in zip(qs, ks, d_logs, mi2s)]
    c_ends = [_dot_tn(k_end.astype(BF16), v.astype(BF16)) * mbdv_f for k_end, v in zip(k_ends, vs)]
    num_is = [_dot(s_qk.astype(BF16), _block_diag(v, mbdv)) for s_qk, v in zip(s_qks, vs)]
    den_is = [_per_head(s_qk, hm2, (hm5,), jnp.sum, 0.0)[0] for s_qk in s_qks]

    c_st, n_st, m5, m2 = cst[...], nst[...], m5s[...], m2s[...]
    c_sts, n_sts, m5l = [], [], []
    for rows, c_end, k_end, me5, me2 in zip(chunks, c_ends, k_ends, me5s, me2s):
        c_sts.append(c_st)
        n_sts.append(n_st)
        m5l.append(m5)
        bl5, bl2 = b5[rows.stop - 1:rows.stop, :], b2[rows.stop - 1:rows.stop, :]
        mn5 = jnp.maximum(bl5 + m5, me5)
        mn2 = jnp.maximum(bl2 + m2, me2)
        c_st = c_st * jnp.exp(bl5 + m5 - mn5) + c_end * jnp.exp(me5 - mn5)
        n_st = n_st * jnp.exp(bl2 + m2 - mn2) + jnp.sum(k_end, axis=0, keepdims=True) * jnp.exp(me2 - mn2)
        m5, m2 = mn5, mn2
    cst[...], nst[...], m5s[...], m2s[...] = c_st, n_st, m5, m2
    q_cs = [_dot(q.astype(BF16), c.astype(BF16)) for q, c in zip(qs, c_sts)]
    for rows, q, q_c, n, m, mi5, num_i, den_i in zip(chunks, qs, q_cs, n_sts, m5l, mi5s, num_is, den_is):
        a5 = b5[rows, :] + m
        m_t = jnp.maximum(a5, mi5)
        ea, ei = jnp.exp(a5 - m_t), jnp.exp(mi5 - m_t)
        (qn,) = _per_head(q * n, hm2, (hm5,), jnp.sum, 0.0)
        h = (ea * q_c + ei * num_i) / jnp.maximum(jnp.abs(ea * qn + ei * den_i), jnp.exp(-m_t))
        ob[rows, :] = _sigmoid(qkvo[rows, 2 * kw + vw:]) * h

    acc = (alpha * h_ref[...] + _dot(oa_ref[...].astype(BF16), woa_ref[...])
           + _dot(ob[...].astype(BF16), wob_ref[...]))
    o_ref[...] = _layer_norm(acc, lg_ref[...], lb_ref[...])


def _mlstm_layer(h, o_a, bsz, s, w_in, i_bias, f_bias, w_out_a, w_out_b, ln_g, ln_b, alpha):
    tt = TT_WIDE
    nt = s // tt
    d = h.shape[1]
    kw = MLSTM_HEADS * MLSTM_DK
    vw = MLSTM_HEADS * MLSTM_DV
    main = 2 * kw + 2 * vw
    w_in = jnp.concatenate([w_in[:, :main], _pad_cols(w_in[:, main:main + MLSTM_HEADS]),
                            _pad_cols(w_in[:, main + MLSTM_HEADS:])], axis=1).astype(BF16)
    consts = [w_in, w_out_a.astype(BF16), w_out_b.astype(BF16), ln_g.reshape(1, d), ln_b.reshape(1, d),
              _pad_lanes(i_bias), _pad_lanes(f_bias), _chunk_tri(TT_MIX, True),
              _expand_mat(MLSTM_HEADS, MLSTM_DK), _expand_mat(MLSTM_HEADS, MLSTM_DV), _wide_masks(),
              _bd_mask(kw, MLSTM_DK), _bd_mask(vw, MLSTM_DV),
              _head_masks(MLSTM_HEADS, MLSTM_DK), _head_masks(MLSTM_HEADS, MLSTM_DV)]
    return pl.pallas_call(
        functools.partial(_mlstm_kernel, alpha=alpha),
        grid=(bsz, nt),
        in_specs=[_tile_spec(tt, d, nt), _tile_spec(tt, o_a.shape[1], nt)]
                 + [_const_spec(c.shape, 2) for c in consts],
        out_specs=_tile_spec(tt, d, nt),
        out_shape=jax.ShapeDtypeStruct((bsz * s, d), F32),
        scratch_shapes=[pltpu.VMEM((tt, main), F32), pltpu.VMEM((tt, vw), F32)]
                       + [pltpu.VMEM((tt, kw), F32)] * 2 + [pltpu.VMEM((tt, vw), F32)] * 2
                       + [pltpu.VMEM((kw, vw), F32), pltpu.VMEM((1, kw), F32),
                          pltpu.VMEM((1, vw), F32), pltpu.VMEM((1, kw), F32)],
        compiler_params=_cparams("parallel", "arbitrary"),
        name="mlstm_layer",
    )(h, o_a, *consts)


def _unit_lower_inverse(ls, wm_ref, mbd):
    def mm(xs, ys):
        return [_dot(a.astype(BF16), _block_diag(b, mbd)) for a, b in zip(xs, ys)]

    ident = wm_ref[M_IDENT]
    d8 = wm_ref[M_D8]
    l8 = [l * d8 for l in ls]
    l8_2 = mm(l8, l8)
    l8_4 = mm(l8_2, l8_2)
    t = mm(mm([ident - a for a in l8], [ident + a for a in l8_2]), [ident + a for a in l8_4])
    for level in (M_O16, M_O32, M_O64):
        off = wm_ref[level]
        corr = mm(mm(t, [l * off for l in ls]), t)
        t = [a - b for a, b in zip(t, corr)]
    return t


def _gdn_kernel(h_ref, win_ref, cw_ref, alog_ref, dtb_ref, nw_ref,
                tri_ref, e_ref, wm_ref, mbd_ref, mbd2_ref,
                o_ref, zbuf, xbuf, xc, gcw, btw, u0b, wdb, atb, qdb, kdb, ob, sst):
    tt = h_ref.shape[0]
    hw = GDN_HEADS * GDN_DK
    n_groups = hw // GROUP

    @pl.when(pl.program_id(1) == 0)
    def _():
        xbuf[0:SUBLANES, :] = jnp.zeros((SUBLANES, 3 * hw), F32)
        sst[...] = jnp.zeros_like(sst)

    hb = h_ref[...].astype(BF16)
    _project_conv_silu(hb, win_ref, 0, xbuf, cw_ref, None, xc, tt, 3 * hw)
    gates = _dot(hb, win_ref[:, 4 * hw:])
    beta = _sigmoid(gates[:, 0:LANES])
    g = -jnp.exp(alog_ref[...]) * _softplus(gates[:, LANES:] + dtb_ref[...])
    zbuf[...] = _dot(hb, win_ref[:, 3 * hw:4 * hw])

    for j, s in ((0, GDN_DK ** -0.5), (1, 1.0)):
        sl = slice(j * hw, (j + 1) * hw)
        x = xc[:, sl]
        xc[:, sl] = x * (lax.rsqrt(_head_sums(x * x, GDN_DK) + L2_EPS) * s)

    gcw[...] = _sel_cols(_sel_rows(tri_ref, g), e_ref)
    btw[...] = _sel_cols(beta, e_ref)

    ident = wm_ref[M_IDENT]
    causal = wm_ref[M_CAUSAL] > 0.5
    strict = wm_ref[M_STRICT]
    mbd_f = mbd_ref[...]
    mbd = mbd_f.astype(BF16)
    mbd2 = mbd2_ref[...].astype(BF16)
    groups = range(n_groups)
    lanes = [slice(grp * GROUP, (grp + 1) * GROUP) for grp in groups]

    tiles = [(slice(c * CHUNK, (c + 1) * CHUNK), grp) for c in range(tt // CHUNK) for grp in groups]
    kbs, kbds, decs, lowers = [], [], [], []
    for rows, grp in tiles:
        l = lanes[grp]
        gc = gcw[rows, l]
        k = xc[rows, hw + grp * GROUP:hw + (grp + 1) * GROUP]
        g_row = jnp.sum(gc * ident, axis=0, keepdims=True)
        decs.append(jnp.exp(jnp.where(causal, gc - g_row, NEG)))
        kbs.append(k * btw[rows, l])
        kbds.append(_block_diag(k, mbd))
        kdb[rows, l] = k * jnp.exp(gc[CHUNK - 1:CHUNK, :] - gc)
    for kb, kbd, dec in zip(kbs, kbds, decs):
        lowers.append(_dot_nt(kb.astype(BF16), kbd) * dec * strict)
    for (rows, grp), kbd, dec in zip(tiles, kbds, decs):
        q = xc[rows, lanes[grp]]
        atb[rows, lanes[grp]] = _dot_nt(q.astype(BF16), kbd) * dec
    t_invs = _unit_lower_inverse(lowers, wm_ref, mbd)
    rbds = []
    for (rows, grp), kb in zip(tiles, kbs):
        l = lanes[grp]
        eg = jnp.exp(gcw[rows, l])
        v = xc[rows, 2 * hw + grp * GROUP:2 * hw + (grp + 1) * GROUP]
        rbds.append(_block_diag(jnp.concatenate([v * btw[rows, l], kb * eg], axis=1), mbd2))
        qdb[rows, l] = xc[rows, l] * eg
    for (rows, grp), t_inv, rbd in zip(tiles, t_invs, rbds):
        sol = _dot(t_inv.astype(BF16), rbd)
        u0b[rows, lanes[grp]] = sol[:, :GROUP]
        wdb[rows, lanes[grp]] = sol[:, GROUP:]

    def chunk(c, carry):
        r0 = pl.multiple_of(c * CHUNK, CHUNK)
        rows = pl.ds(r0, CHUNK)
        sts = [sst[grp] for grp in groups]
        rs = [_dot(jnp.concatenate([wdb[rows, l], qdb[rows, l]], axis=0).astype(BF16), st.astype(BF16))
              for l, st in zip(lanes, sts)]
        us = [u0b[rows, l] - r[:CHUNK] for l, r in zip(lanes, rs)]
        intra = [_dot(atb[rows, l].astype(BF16), _block_diag(u, mbd)) for l, u in zip(lanes, us)]
        upd = [_dot_tn(kdb[rows, l].astype(BF16), u.astype(BF16)) for l, u in zip(lanes, us)]
        for grp in groups:
            l = lanes[grp]
            ob[rows, l] = rs[grp][CHUNK:] + intra[grp]
            g_last = gcw[pl.ds(r0 + CHUNK - 1, 1), l]
            sst[grp] = sts[grp] * jnp.exp(g_last) + upd[grp] * mbd_f
        return carry

    lax.fori_loop(0, tt // CHUNK, chunk, 0, unroll=True)

    o = ob[...]
    ms = _head_sums(o * o, GDN_DV) * (1.0 / GDN_DV)
    o_ref[...] = o * lax.rsqrt(ms + RMS_EPS) * nw_ref[...] * _silu(zbuf[...])


def _gdn_mixer(h, bsz, s, w_in, conv_w, a_log, dt_bias, norm_w):
    tt = TT_WIDE
    nt = s // tt
    d = h.shape[1]
    hw = GDN_HEADS * GDN_DK
    w_in = jnp.concatenate([w_in[:, :4 * hw], _pad_cols(w_in[:, 4 * hw:4 * hw + GDN_HEADS]),
                            _pad_cols(w_in[:, 4 * hw + GDN_HEADS:])], axis=1).astype(BF16)
    consts = [w_in, conv_w.astype(F32), _pad_lanes(a_log), _pad_lanes(dt_bias),
              jnp.tile(norm_w.astype(F32), GDN_HEADS).reshape(1, hw),
              _chunk_tri(TT_MIX, True), _expand_mat(GDN_HEADS, GDN_DK), _wide_masks(),
              _bd_mask(GROUP, CHUNK), _bd_mask(2 * GROUP, CHUNK)]
    return pl.pallas_call(
        _gdn_kernel,
        grid=(bsz, nt),
        in_specs=[_tile_spec(tt, d, nt)] + [_const_spec(c.shape, 2) for c in consts],
        out_specs=_tile_spec(tt, hw, nt),
        out_shape=jax.ShapeDtypeStruct((bsz * s, hw), F32),
        scratch_shapes=[pltpu.VMEM((tt, hw), F32),
                        pltpu.VMEM((tt + SUBLANES, 3 * hw), F32), pltpu.VMEM((tt, 3 * hw), F32)]
                       + [pltpu.VMEM((tt, hw), F32)] * 8
                       + [pltpu.VMEM((hw // GROUP, GROUP, GROUP), F32)],
        compiler_params=_cparams("parallel", "arbitrary"),
        name="gdn_mixer",
    )(h, *consts)


def kernel(x, ffn_pre_w_gate, ffn_pre_w_up, ffn_pre_w_down, ln_pre_g, ln_pre_b, hyb_w_in, hyb_conv_w, gdn_a_log, gdn_dt_bias, gdn_norm_w, mlstm_i_bias, mlstm_f_bias, hyb_w_out, ssd_w_in, ssd_conv_w, ssd_conv_b, ssd_a_log, ssd_dt_bias, ssd_d_skip, ssd_norm_w, ssd_w_out, ln_mix_g, ln_mix_b, ffn_post_w_gate, ffn_post_w_up, ffn_post_w_down, ln_post_g, ln_post_b):
    bsz, s, d = x.shape
    depth = ffn_pre_w_gate.shape[0]
    alpha = (2 * depth) ** 0.25
    gdn_in = 4 * GDN_HEADS * GDN_DK + 2 * GDN_HEADS
    gdn_out = GDN_HEADS * GDN_DV
    pre = [w.astype(BF16) for w in (ffn_pre_w_gate, ffn_pre_w_up, ffn_pre_w_down)]
    post = [w.astype(BF16) for w in (ffn_post_w_gate, ffn_post_w_up, ffn_post_w_down)]
    h = x.reshape(bsz * s, d)
    for l in range(depth):
        h = _ffn(h, *pre, ln_pre_g, ln_pre_b, l, alpha)
        j = l // 2
        if l % 2 == 0:
            o_a = _gdn_mixer(h, bsz, s, hyb_w_in[j][:, :gdn_in], hyb_conv_w[j], gdn_a_log[j], gdn_dt_bias[j],
                             gdn_norm_w[j])
            h = _mlstm_layer(h, o_a, bsz, s, hyb_w_in[j][:, gdn_in:], mlstm_i_bias[j], mlstm_f_bias[j],
                             hyb_w_out[j][:gdn_out], hyb_w_out[j][gdn_out:], ln_mix_g[l], ln_mix_b[l], alpha)
        else:
            h = _ssd_layer(h, bsz, s, ssd_w_in[j], ssd_conv_w[j], ssd_conv_b[j], ssd_a_log[j], ssd_dt_bias[j],
                           ssd_d_skip[j], ssd_norm_w[j], ssd_w_out[j], ln_mix_g[l], ln_mix_b[l], alpha)
        h = _ffn(h, *post, ln_post_g, ln_post_b, l, alpha)
    return h.reshape(bsz, s, d)
```

```python
import functools

import jax
import jax.numpy as jnp
import numpy as np
from jax import lax
from jax.experimental import pallas as pl
from jax.experimental.pallas import tpu as pltpu

F32 = jnp.float32
BF16 = jnp.bfloat16

LN_EPS = 1e-5
RMS_EPS = 1e-6
L2_EPS = 1e-6
CHUNK = 64
CONV_K = 4
NEG = -1e30

GDN_HEADS, GDN_DK, GDN_DV = 8, 64, 64
MLSTM_HEADS, MLSTM_DK, MLSTM_DV = 4, 64, 128
SSD_HEADDIM, SSD_GROUPS, SSD_STATE = 64, 4, 128

LANES = 128
SUBLANES = 8
VMEM_LIMIT = 56 * 1024 * 1024
GROUP = 4 * CHUNK

TM_DENSE = 512
TT_MIX = 256
TT_WIDE = 512


def _cparams(*sem):
    return pltpu.CompilerParams(dimension_semantics=sem, vmem_limit_bytes=VMEM_LIMIT)


def _dot(a, b):
    return jnp.dot(a, b, preferred_element_type=F32)


def _dot_nt(a, b):
    return lax.dot_general(a, b, (((1,), (1,)), ((), ())), preferred_element_type=F32)


def _dot_tn(a, b):
    return lax.dot_general(a, b, (((0,), (0,)), ((), ())), preferred_element_type=F32)


def _layer_norm(y, g, b):
    mu = jnp.mean(y, -1, keepdims=True)
    yc = y - mu
    var = jnp.mean(yc * yc, -1, keepdims=True)
    return yc * lax.rsqrt(var + LN_EPS) * g + b


def _sigmoid(x):
    return 1.0 / (1.0 + jnp.exp(-x))


def _silu(x):
    return x * _sigmoid(x)


def _softplus(x):
    return jnp.maximum(x, 0.0) + jnp.log(1.0 + jnp.exp(-jnp.abs(x)))


def _split(x, parts):
    out = []
    r = x
    for p in range(parts):
        h = r.astype(BF16)
        out.append(h)
        if p + 1 < parts:
            r = r - h.astype(F32)
    return out


def _sel_rows(m01x3_ref, x):
    rows = m01x3_ref.shape[0]
    return jnp.concatenate([_dot(m01x3_ref[...], jnp.concatenate(_split(x[r:r + rows], 3), axis=0))
                            for r in range(0, x.shape[0], rows)], axis=0)


def _sel_cols(x, m01x3_ref, parts=3):
    return _dot(jnp.concatenate(_split(x, parts), axis=1), m01x3_ref[...])


def _head_sums(x, head_width):
    low = lax.broadcasted_iota(jnp.int32, (1, LANES), 1) < head_width
    out = []
    for start in range(0, x.shape[1], LANES):
        blk = x[:, start:start + LANES]
        s_low = jnp.sum(jnp.where(low, blk, 0.0), axis=-1, keepdims=True)
        s_high = jnp.sum(jnp.where(low, 0.0, blk), axis=-1, keepdims=True)
        out.append(jnp.where(low, s_low, s_high))
    return jnp.concatenate(out, axis=1)


def _tile4(x):
    return jnp.concatenate([x, x, x, x], axis=0)


def _block_diag(x, mask_bf16):
    return _tile4(x.astype(BF16)) * mask_bf16


def _ffn_kernel(x_ref, wg_ref, wu_ref, wd_ref, g_ref, b_ref, o_ref, *, fc, alpha):
    x = x_ref[...]
    xb = x.astype(BF16)
    acc = jnp.zeros_like(x)
    for c in range(wg_ref.shape[1] // fc):
        sl = slice(c * fc, (c + 1) * fc)
        hg = _dot(xb, wg_ref[:, sl].astype(BF16))
        hu = _dot(xb, wu_ref[:, sl].astype(BF16))
        acc = acc + _dot((_silu(hg) * hu).astype(BF16), wd_ref[sl, :].astype(BF16))
    o_ref[...] = _layer_norm(alpha * x + 0.5 * acc, g_ref[...], b_ref[...])


def _const_spec(shape, grid_rank=1):
    zeros = (0,) * len(shape)
    if grid_rank == 1:
        return pl.BlockSpec(shape, lambda i: zeros, pipeline_mode=pl.Buffered(1))
    return pl.BlockSpec(shape, lambda b, i: zeros, pipeline_mode=pl.Buffered(1))


def _ffn(x, wg, wu, wd, g, b, layer, alpha):
    t, d = x.shape
    f = wg.shape[2]
    tm = TM_DENSE

    def layer_spec(rows, cols):
        return pl.BlockSpec((None, rows, cols), lambda i: (layer, 0, 0), pipeline_mode=pl.Buffered(1))

    return pl.pallas_call(
        functools.partial(_ffn_kernel, fc=2 * LANES, alpha=alpha),
        grid=(t // tm,),
        in_specs=[pl.BlockSpec((tm, d), lambda i: (i, 0)),
                  layer_spec(d, f), layer_spec(d, f), layer_spec(f, d),
                  _const_spec((1, d)), _const_spec((1, d))],
        out_specs=pl.BlockSpec((tm, d), lambda i: (i, 0)),
        out_shape=jax.ShapeDtypeStruct((t, d), F32),
        compiler_params=_cparams("parallel"),
        name="ffn",
    )(x, wg, wu, wd, g[layer].reshape(1, d), b[layer].reshape(1, d))


def _chunk_tri(tt, inclusive_lower):
    i = np.arange(tt)[:, None]
    j = np.arange(tt)[None, :]
    same = (i // CHUNK) == (j // CHUNK)
    m = same & (j <= i) if inclusive_lower else same
    return jnp.asarray(np.tile(m.astype(np.float32), (1, 3)), BF16)


def _expand_mat(n_heads, width, parts=3):
    e = np.zeros((LANES, n_heads * width), np.float32)
    for h in range(n_heads):
        e[h, h * width:(h + 1) * width] = 1.0
    return jnp.asarray(np.tile(e, (parts, 1)), BF16)


def _wide_masks():
    i = np.arange(CHUNK)[:, None]
    j = (np.arange(GROUP) % CHUNK)[None, :]
    ident = i == j
    causal = i >= j
    strict = i > j
    d8 = (i // 8) == (j // 8)
    o16 = ((i // 16) == (j // 16)) & ((i // 8) == (j // 8) + 1)
    o32 = ((i // 32) == (j // 32)) & ((i // 16) == (j // 16) + 1)
    o64 = (i // 32) == (j // 32) + 1
    return jnp.asarray(np.stack([ident, causal, strict, d8, o16, o32, o64]).astype(np.float32))


M_IDENT, M_CAUSAL, M_STRICT, M_D8, M_O16, M_O32, M_O64 = range(7)


def _bd_mask(n_cols, col_block):
    r = np.arange(GROUP)[:, None] // CHUNK
    c = (np.arange(n_cols)[None, :] % (4 * col_block)) // col_block
    return jnp.asarray((r == c).astype(np.float32))


def _head_masks(n_heads, width):
    m = np.zeros((SUBLANES, n_heads * width), np.float32)
    for h in range(n_heads):
        m[h, h * width:(h + 1) * width] = 1.0
    return jnp.asarray(m)


def _pad_cols(w):
    return jnp.pad(w, ((0, 0), (0, LANES - w.shape[1])))


def _pad_lanes(v):
    v = v.reshape(1, -1).astype(F32)
    return jnp.pad(v, ((0, 0), (0, LANES - v.shape[1])))


def _project_conv_silu(xb, w_ref, col0, halo, cw_ref, cb_ref, dst, tt, width):
    for start in range(0, width, 4 * LANES):
        sl = slice(start, min(start + 4 * LANES, width))
        p = _dot(xb, w_ref[:, col0 + sl.start:col0 + sl.stop])
        full = jnp.concatenate([halo[:, sl], p], axis=0)
        acc = cw_ref[CONV_K - 1:CONV_K, sl] * p
        for k in range(CONV_K - 1):
            acc = acc + cw_ref[k:k + 1, sl] * pltpu.roll(full, CONV_K - 1 - k, axis=0)[SUBLANES:SUBLANES + tt]
        if cb_ref is not None:
            acc = acc + cb_ref[:, sl]
        dst[:, sl] = _silu(acc)
        halo[:, sl] = p[tt - SUBLANES:tt]


def _tile_spec(tt, width, nt):
    return pl.BlockSpec((tt, width), lambda b, i: (b * nt + i, 0))


def _ssd_kernel(h_ref, win_ref, wout_ref, lg_ref, lb_ref,
                cw_ref, cb_ref, alog_ref, dtb_ref, dskip_ref, nw_ref, tri_ref, e_ref, wm_ref, mbd_ref,
                o_ref, zbuf, halo, xc, acol, xdt, hst, *, alpha):
    tt = h_ref.shape[0]
    ncx = zbuf.shape[1]
    ncb = halo.shape[1] - ncx
    gw = ncx // SSD_GROUPS

    @pl.when(pl.program_id(1) == 0)
    def _():
        halo[...] = jnp.zeros_like(halo)
        hst[...] = jnp.zeros_like(hst)

    hb = h_ref[...].astype(BF16)
    _project_conv_silu(hb, win_ref, ncx, halo, cw_ref, cb_ref, xc, tt, ncx + ncb)

    dt = _softplus(_dot(hb, win_ref[:, 2 * ncx + ncb:]) + dtb_ref[...])
    a_cum = _sel_rows(tri_ref, dt * (-jnp.exp(alog_ref[...])))
    for start in range(0, ncx, 4 * LANES):
        sl = slice(start, start + 4 * LANES)
        zbuf[:, sl] = _silu(_dot(hb, win_ref[:, sl]))
        acol[:, sl] = _sel_cols(a_cum, e_ref.at[:, sl])
        xdt[:, sl] = xc[:, sl] * _sel_cols(dt, e_ref.at[0:2 * LANES, sl], 2)

    ident = wm_ref[M_IDENT]
    causal = wm_ref[M_CAUSAL] > 0.5
    mbd = mbd_ref[...].astype(BF16)
    groups = range(SSD_GROUPS)
    halves = range(gw // GROUP)

    def chunk(c, carry):
        r0 = pl.multiple_of(c * CHUNK, CHUNK)
        rows = pl.ds(r0, CHUNK)
        gls = [slice(g * gw, (g + 1) * gw) for g in groups]
        b_gs = [xc[rows, ncx + g * SSD_STATE:ncx + (g + 1) * SSD_STATE].astype(BF16) for g in groups]
        c_gs = [xc[rows, ncx + ncb // 2 + g * SSD_STATE:ncx + ncb // 2 + (g + 1) * SSD_STATE].astype(BF16)
                for g in groups]
        cb_wide = [_dot_nt(c_gs[g], _tile4(b_gs[g])) for g in groups]
        a_gs = [acol[rows, gl] for gl in gls]
        a_lasts = [acol[pl.ds(r0 + CHUNK - 1, 1), gl] for gl in gls]
        x_gs = [xdt[rows, gl] for gl in gls]
        lhs, rhs = [], []
        for g in groups:
            for hh in halves:
                hl = slice(hh * GROUP, (hh + 1) * GROUP)
                a_c = a_gs[g][:, hl]
                a_row = jnp.sum(a_c * ident, axis=0, keepdims=True)
                seg = jnp.exp(jnp.where(causal, a_c - a_row, NEG))
                lhs.append((cb_wide[g] * seg).astype(BF16))
                rhs.append(_block_diag(x_gs[g][:, hl], mbd))
        y_diag = [_dot(a, b) for a, b in zip(lhs, rhs)]
        h_prev = [hst[g] for g in groups]
        y_off = [_dot(c_gs[g], h_prev[g].astype(BF16)) for g in groups]
        upd = [_dot_tn(b_gs[g], (x_gs[g] * jnp.exp(a_lasts[g] - a_gs[g])).astype(BF16)) for g in groups]
        for g in groups:
            gl = gls[g]
            hst[g] = h_prev[g] * jnp.exp(a_lasts[g]) + upd[g]
            y = jnp.concatenate(y_diag[g * len(halves):(g + 1) * len(halves)], axis=1)
            y = y + y_off[g] * jnp.exp(a_gs[g]) + dskip_ref[:, gl] * xc[rows, gl]
            y = y * zbuf[rows, gl]
            xdt[rows, gl] = y * lax.rsqrt(jnp.mean(y * y, -1, keepdims=True) + RMS_EPS) * nw_ref[:, gl]
        return carry

    lax.fori_loop(0, tt // CHUNK, chunk, 0, unroll=True)
    acc = alpha * h_ref[...] + _dot(xdt[...].astype(BF16), wout_ref[...])
    o_ref[...] = _layer_norm(acc, lg_ref[...], lb_ref[...])


def _ssd_layer(h, bsz, s, w_in, conv_w, conv_b, a_log, dt_bias, d_skip, norm_w, w_out, ln_g, ln_b, alpha):
    tt = TT_MIX
    nt = s // tt
    d = h.shape[1]
    n_heads = a_log.shape[0]
    ncx = n_heads * SSD_HEADDIM
    ncb = 2 * SSD_GROUPS * SSD_STATE
    w_in = jnp.concatenate([w_in[:, :2 * ncx + ncb], _pad_cols(w_in[:, 2 * ncx + ncb:])], axis=1).astype(BF16)
    consts = [w_in, w_out.astype(BF16), ln_g.reshape(1, d), ln_b.reshape(1, d),
              conv_w.astype(F32), conv_b.reshape(1, -1).astype(F32), _pad_lanes(a_log), _pad_lanes(dt_bias),
              jnp.repeat(d_skip.astype(F32), SSD_HEADDIM).reshape(1, ncx), norm_w.reshape(1, ncx).astype(F32),
              _chunk_tri(tt, True), _expand_mat(n_heads, SSD_HEADDIM), _wide_masks(), _bd_mask(GROUP, CHUNK)]
    return pl.pallas_call(
        functools.partial(_ssd_kernel, alpha=alpha),
        grid=(bsz, nt),
        in_specs=[_tile_spec(tt, d, nt)] + [_const_spec(c.shape, 2) for c in consts],
        out_specs=_tile_spec(tt, d, nt),
        out_shape=jax.ShapeDtypeStruct((bsz * s, d), F32),
        scratch_shapes=[pltpu.VMEM((tt, ncx), F32),
                        pltpu.VMEM((SUBLANES, ncx + ncb), F32), pltpu.VMEM((tt, ncx + ncb), F32),
                        pltpu.VMEM((tt, ncx), F32), pltpu.VMEM((tt, ncx), F32),
                        pltpu.VMEM((SSD_GROUPS, SSD_STATE, ncx // SSD_GROUPS), F32)],
        compiler_params=_cparams("parallel", "arbitrary"),
        name="ssd_layer",
    )(h, *consts)


def _per_head(x, hm_in, hm_outs, reduce_fn, fill):
    outs = [None] * len(hm_outs)
    for h in range(MLSTM_HEADS):
        r = reduce_fn(jnp.where(hm_in[h:h + 1, :] > 0.5, x, fill), axis=-1, keepdims=True)
        outs = [jnp.broadcast_to(r, (x.shape[0], hm.shape[1])) if o is None
                else jnp.where(hm[h:h + 1, :] > 0.5, r, o) for o, hm in zip(outs, hm_outs)]
    return outs


def _mlstm_kernel(h_ref, oa_ref, win_ref, woa_ref, wob_ref, lg_ref, lb_ref, ib_ref, fb_ref,
                  tri_ref, e2_ref, e5_ref, wm_ref, mbd_ref, mbdv_ref, hm2_ref, hm5_ref,
                  o_ref, qkvo, ob, b2, li2, b5, li5, cst, nst, m5s, m2s, *, alpha):
    tt = h_ref.shape[0]
    kw = MLSTM_HEADS * MLSTM_DK
    vw = MLSTM_HEADS * MLSTM_DV
    scale = MLSTM_DK ** -0.5

    @pl.when(pl.program_id(1) == 0)
    def _():
        cst[...] = jnp.zeros_like(cst)
        nst[...] = jnp.zeros_like(nst)
        m5s[...] = jnp.zeros_like(m5s)
        m2s[...] = jnp.zeros_like(m2s)

    hb = h_ref[...].astype(BF16)
    gates = _dot(hb, win_ref[:, 2 * kw + 2 * vw:])
    log_i = gates[:, 0:LANES] + ib_ref[...]
    log_f = -_softplus(-(gates[:, LANES:] + fb_ref[...]))
    for start in range(0, 2 * kw + 2 * vw, 4 * LANES):
        sl = slice(start, start + 4 * LANES)
        qkvo[:, sl] = _dot(hb, win_ref[:, sl])

    b = _sel_rows(tri_ref, log_f)
    for src, d2, d5 in ((b, b2, b5), (log_i, li2, li5)):
        d2[...] = _sel_cols(src, e2_ref)
        d5[...] = _sel_cols(src, e5_ref)

    ident = wm_ref[M_IDENT]
    causal = wm_ref[M_CAUSAL] > 0.5
    mbd = mbd_ref[...].astype(BF16)
    mbdv_f = mbdv_ref[...]
    mbdv = mbdv_f.astype(BF16)
    hm2 = hm2_ref[...]
    hm5 = hm5_ref[...]
    chunks = [slice(c * CHUNK, (c + 1) * CHUNK) for c in range(tt // CHUNK)]

    qs = [qkvo[rows, 0:kw] for rows in chunks]
    ks = [qkvo[rows, kw:2 * kw] * scale for rows in chunks]
    vs = [qkvo[rows, 2 * kw:2 * kw + vw] for rows in chunks]
    d_logs, mi2s, mi5s, me2s, me5s, k_ends = [], [], [], [], [], []
    for rows, k in zip(chunks, ks):
        bc2, lic2 = b2[rows, :], li2[rows, :]
        row_v = jnp.sum((lic2 - bc2) * ident, axis=0, keepdims=True)
        d_log = jnp.where(causal, bc2 + row_v, NEG)
        mi2, mi5 = _per_head(d_log, hm2, (hm2, hm5), jnp.max, NEG)
        d_logs.append(d_log)
        mi2s.append(mi2)
        mi5s.append(mi5)
        bl2, bl5 = b2[rows.stop - 1:rows.stop, :], b5[rows.stop - 1:rows.stop, :]
        w2 = bl2 - bc2 + lic2
        me2 = jnp.max(w2, axis=0, keepdims=True)
        me2s.append(me2)
        me5s.append(jnp.max(bl5 - b5[rows, :] + li5[rows, :], axis=0, keepdims=True))
        k_ends.append(k * jnp.exp(w2 - me2))
    s_qks = [_dot_nt(q.astype(BF16), _block_diag(k, mbd)) * jnp.exp(d_log - mi2)
             for q, k, d_log, mi2 in zip(qs, ks, d_logs, mi2s)]
    c_ends = [_dot_tn(k_end.astype(BF16), v.astype(BF16)) * mbdv_f for k_end, v in zip(k_ends, vs)]
    num_is = [_dot(s_qk.astype(BF16), _block_diag(v, mbdv)) for s_qk, v in zip(s_qks, vs)]
    den_is = [_per_head(s_qk, hm2, (hm5,), jnp.sum, 0.0)[0] for s_qk in s_qks]

    c_st, n_st, m5, m2 = cst[...], nst[...], m5s[...], m2s[...]
    c_sts, n_sts, m5l = [], [], []
    for rows, c_end, k_end, me5, me2 in zip(chunks, c_ends, k_ends, me5s, me2s):
        c_sts.append(c_st)
        n_sts.append(n_st)
        m5l.append(m5)
        bl5, bl2 = b5[rows.stop - 1:rows.stop, :], b2[rows.stop - 1:rows.stop, :]
        mn5 = jnp.maximum(bl5 + m5, me5)
        mn2 = jnp.maximum(bl2 + m2, me2)
        c_st = c_st * jnp.exp(bl5 + m5 - mn5) + c_end * jnp.exp(me5 - mn5)
        n_st = n_st * jnp.exp(bl2 + m2 - mn2) + jnp.sum(k_end, axis=0, keepdims=True) * jnp.exp(me2 - mn2)
        m5, m2 = mn5, mn2
    cst[...], nst[...], m5s[...], m2s[...] = c_st, n_st, m5, m2
    q_cs = [_dot(q.astype(BF16), c.astype(BF16)) for q, c in zip(qs, c_sts)]
    for rows, q, q_c, n, m, mi5, num_i, den_i in zip(chunks, qs, q_cs, n_sts, m5l, mi5s, num_is, den_is):
        a5 = b5[rows, :] + m
        m_t = jnp.maximum(a5, mi5)
        ea, ei = jnp.exp(a5 - m_t), jnp.exp(mi5 - m_t)
        (qn,) = _per_head(q * n, hm2, (hm5,), jnp.sum, 0.0)
        h = (ea * q_c + ei * num_i) / jnp.maximum(jnp.abs(ea * qn + ei * den_i), jnp.exp(-m_t))
        ob[rows, :] = _sigmoid(qkvo[rows, 2 * kw + vw:]) * h

    acc = (alpha * h_ref[...] + _dot(oa_ref[...].astype(BF16), woa_ref[...])
           + _dot(ob[...].astype(BF16), wob_ref[...]))
    o_ref[...] = _layer_norm(acc, lg_ref[...], lb_ref[...])


def _mlstm_layer(h, o_a, bsz, s, w_in, i_bias, f_bias, w_out_a, w_out_b, ln_g, ln_b, alpha):
    tt = TT_WIDE
    nt = s // tt
    d = h.shape[1]
    kw = MLSTM_HEADS * MLSTM_DK
    vw = MLSTM_HEADS * MLSTM_DV
    main = 2 * kw + 2 * vw
    w_in = jnp.concatenate([w_in[:, :main], _pad_cols(w_in[:, main:main + MLSTM_HEADS]),
                            _pad_cols(w_in[:, main + MLSTM_HEADS:])], axis=1).astype(BF16)
    consts = [w_in, w_out_a.astype(BF16), w_out_b.astype(BF16), ln_g.reshape(1, d), ln_b.reshape(1, d),
              _pad_lanes(i_bias), _pad_lanes(f_bias), _chunk_tri(TT_MIX, True),
              _expand_mat(MLSTM_HEADS, MLSTM_DK), _expand_mat(MLSTM_HEADS, MLSTM_DV), _wide_masks(),
              _bd_mask(kw, MLSTM_DK), _bd_mask(vw, MLSTM_DV),
              _head_masks(MLSTM_HEADS, MLSTM_DK), _head_masks(MLSTM_HEADS, MLSTM_DV)]
    return pl.pallas_call(
        functools.partial(_mlstm_kernel, alpha=alpha),
        grid=(bsz, nt),
        in_specs=[_tile_spec(tt, d, nt), _tile_spec(tt, o_a.shape[1], nt)]
                 + [_const_spec(c.shape, 2) for c in consts],
        out_specs=_tile_spec(tt, d, nt),
        out_shape=jax.ShapeDtypeStruct((bsz * s, d), F32),
        scratch_shapes=[pltpu.VMEM((tt, main), F32), pltpu.VMEM((tt, vw), F32)]
                       + [pltpu.VMEM((tt, kw), F32)] * 2 + [pltpu.VMEM((tt, vw), F32)] * 2
                       + [pltpu.VMEM((kw, vw), F32), pltpu.VMEM((1, kw), F32),
                          pltpu.VMEM((1, vw), F32), pltpu.VMEM((1, kw), F32)],
        compiler_params=_cparams("parallel", "arbitrary"),
        name="mlstm_layer",
    )(h, o_a, *consts)


def _unit_lower_inverse(ls, wm_ref, mbd):
    def mm(xs, ys):
        return [_dot(a.astype(BF16), _block_diag(b, mbd)) for a, b in zip(xs, ys)]

    ident = wm_ref[M_IDENT]
    d8 = wm_ref[M_D8]
    l8 = [l * d8 for l in ls]
    l8_2 = mm(l8, l8)
    l8_4 = mm(l8_2, l8_2)
    t = mm(mm([ident - a for a in l8], [ident + a for a in l8_2]), [ident + a for a in l8_4])
    for level in (M_O16, M_O32, M_O64):
        off = wm_ref[level]
        corr = mm(mm(t, [l * off for l in ls]), t)
        t = [a - b for a, b in zip(t, corr)]
    return t


def _gdn_kernel(h_ref, win_ref, cw_ref, alog_ref, dtb_ref, nw_ref,
                tri_ref, e_ref, wm_ref, mbd_ref, mbd2_ref,
                o_ref, zbuf, halo, xc, gcw, btw, u0b, wdb, atb, qdb, kdb, ob, sst):
    tt = h_ref.shape[0]
    hw = GDN_HEADS * GDN_DK
    n_groups = hw // GROUP

    @pl.when(pl.program_id(1) == 0)
    def _():
        halo[...] = jnp.zeros_like(halo)
        sst[...] = jnp.zeros_like(sst)

    hb = h_ref[...].astype(BF16)
    _project_conv_silu(hb, win_ref, 0, halo, cw_ref, None, xc, tt, 3 * hw)
    gates = _dot(hb, win_ref[:, 4 * hw:])
    beta = _sigmoid(gates[:, 0:LANES])
    g = -jnp.exp(alog_ref[...]) * _softplus(gates[:, LANES:] + dtb_ref[...])
    zbuf[...] = _dot(hb, win_ref[:, 3 * hw:4 * hw])

    for j, s in ((0, GDN_DK ** -0.5), (1, 1.0)):
        sl = slice(j * hw, (j + 1) * hw)
        x = xc[:, sl]
        xc[:, sl] = x * (lax.rsqrt(_head_sums(x * x, GDN_DK) + L2_EPS) * s)

    gcw[...] = _sel_cols(_sel_rows(tri_ref, g), e_ref)
    btw[...] = _sel_cols(beta, e_ref)

    ident = wm_ref[M_IDENT]
    causal = wm_ref[M_CAUSAL] > 0.5
    strict = wm_ref[M_STRICT]
    mbd_f = mbd_ref[...]
    mbd = mbd_f.astype(BF16)
    mbd2 = mbd2_ref[...].astype(BF16)
    groups = range(n_groups)
    lanes = [slice(grp * GROUP, (grp + 1) * GROUP) for grp in groups]

    tiles = [(slice(c * CHUNK, (c + 1) * CHUNK), grp) for c in range(tt // CHUNK) for grp in groups]
    kbs, kbds, decs, lowers = [], [], [], []
    for rows, grp in tiles:
        l = lanes[grp]
        gc = gcw[rows, l]
        k = xc[rows, hw + grp * GROUP:hw + (grp + 1) * GROUP]
        g_row = jnp.sum(gc * ident, axis=0, keepdims=True)
        decs.append(jnp.exp(jnp.where(causal, gc - g_row, NEG)))
        kbs.append(k * btw[rows, l])
        kbds.append(_block_diag(k, mbd))
        kdb[rows, l] = k * jnp.exp(gc[CHUNK - 1:CHUNK, :] - gc)
    for kb, kbd, dec in zip(kbs, kbds, decs):
        lowers.append(_dot_nt(kb.astype(BF16), kbd) * dec * strict)
    for (rows, grp), kbd, dec in zip(tiles, kbds, decs):
        q = xc[rows, lanes[grp]]
        atb[rows, lanes[grp]] = _dot_nt(q.astype(BF16), kbd) * dec
    t_invs = _unit_lower_inverse(lowers, wm_ref, mbd)
    rbds = []
    for (rows, grp), kb in zip(tiles, kbs):
        l = lanes[grp]
        eg = jnp.exp(gcw[rows, l])
        v = xc[rows, 2 * hw + grp * GROUP:2 * hw + (grp + 1) * GROUP]
        rbds.append(_block_diag(jnp.concatenate([v * btw[rows, l], kb * eg], axis=1), mbd2))
        qdb[rows, l] = xc[rows, l] * eg
    for (rows, grp), t_inv, rbd in zip(tiles, t_invs, rbds):
        sol = _dot(t_inv.astype(BF16), rbd)
        u0b[rows, lanes[grp]] = sol[:, :GROUP]
        wdb[rows, lanes[grp]] = sol[:, GROUP:]

    def chunk(c, carry):
        r0 = pl.multiple_of(c * CHUNK, CHUNK)
        rows = pl.ds(r0, CHUNK)
        sts = [sst[grp] for grp in groups]
        rs = [_dot(jnp.concatenate([wdb[rows, l], qdb[rows, l]], axis=0).astype(BF16), st.astype(BF16))
              for l, st in zip(lanes, sts)]
        us = [u0b[rows, l] - r[:CHUNK] for l, r in zip(lanes, rs)]
        intra = [_dot(atb[rows, l].astype(BF16), _block_diag(u, mbd)) for l, u in zip(lanes, us)]
        upd = [_dot_tn(kdb[rows, l].astype(BF16), u.astype(BF16)) for l, u in zip(lanes, us)]
        for grp in groups:
            l = lanes[grp]
            ob[rows, l] = rs[grp][CHUNK:] + intra[grp]
            g_last = gcw[pl.ds(r0 + CHUNK - 1, 1), l]
            sst[grp] = sts[grp] * jnp.exp(g_last) + upd[grp] * mbd_f
        return carry

    lax.fori_loop(0, tt // CHUNK, chunk, 0, unroll=True)

    o = ob[...]
    ms = _head_sums(o * o, GDN_DV) * (1.0 / GDN_DV)
    o_ref[...] = o * lax.rsqrt(ms + RMS_EPS) * nw_ref[...] * _silu(zbuf[...])


def _gdn_mixer(h, bsz, s, w_in, conv_w, a_log, dt_bias, norm_w):
    tt = TT_WIDE
    nt = s // tt
    d = h.shape[1]
    hw = GDN_HEADS * GDN_DK
    w_in = jnp.concatenate([w_in[:, :4 * hw], _pad_cols(w_in[:, 4 * hw:4 * hw + GDN_HEADS]),
                            _pad_cols(w_in[:, 4 * hw + GDN_HEADS:])], axis=1).astype(BF16)
    consts = [w_in, conv_w.astype(F32), _pad_lanes(a_log), _pad_lanes(dt_bias),
              jnp.tile(norm_w.astype(F32), GDN_HEADS).reshape(1, hw),
              _chunk_tri(TT_MIX, True), _expand_mat(GDN_HEADS, GDN_DK), _wide_masks(),
              _bd_mask(GROUP, CHUNK), _bd_mask(2 * GROUP, CHUNK)]
    return pl.pallas_call(
        _gdn_kernel,
        grid=(bsz, nt),
        in_specs=[_tile_spec(tt, d, nt)] + [_const_spec(c.shape, 2) for c in consts],
        out_specs=_tile_spec(tt, hw, nt),
        out_shape=jax.ShapeDtypeStruct((bsz * s, hw), F32),
        scratch_shapes=[pltpu.VMEM((tt, hw), F32),
                        pltpu.VMEM((SUBLANES, 3 * hw), F32), pltpu.VMEM((tt, 3 * hw), F32)]
                       + [pltpu.VMEM((tt, hw), F32)] * 8
                       + [pltpu.VMEM((hw // GROUP, GROUP, GROUP), F32)],
        compiler_params=_cparams("parallel", "arbitrary"),
        name="gdn_mixer",
    )(h, *consts)


def kernel(x, ffn_pre_w_gate, ffn_pre_w_up, ffn_pre_w_down, ln_pre_g, ln_pre_b, hyb_w_in, hyb_conv_w, gdn_a_log, gdn_dt_bias, gdn_norm_w, mlstm_i_bias, mlstm_f_bias, hyb_w_out, ssd_w_in, ssd_conv_w, ssd_conv_b, ssd_a_log, ssd_dt_bias, ssd_d_skip, ssd_norm_w, ssd_w_out, ln_mix_g, ln_mix_b, ffn_post_w_gate, ffn_post_w_up, ffn_post_w_down, ln_post_g, ln_post_b):
    bsz, s, d = x.shape
    depth = ffn_pre_w_gate.shape[0]
    alpha = (2 * depth) ** 0.25
    gdn_in = 4 * GDN_HEADS * GDN_DK + 2 * GDN_HEADS
    gdn_out = GDN_HEADS * GDN_DV
    pre = (ffn_pre_w_gate, ffn_pre_w_up, ffn_pre_w_down)
    post = (ffn_post_w_gate, ffn_post_w_up, ffn_post_w_down)
    h = x.reshape(bsz * s, d)
    for l in range(depth):
        h = _ffn(h, *pre, ln_pre_g, ln_pre_b, l, alpha)
        j = l // 2
        if l % 2 == 0:
            o_a = _gdn_mixer(h, bsz, s, hyb_w_in[j][:, :gdn_in], hyb_conv_w[j], gdn_a_log[j], gdn_dt_bias[j],
                             gdn_norm_w[j])
            h = _mlstm_layer(h, o_a, bsz, s, hyb_w_in[j][:, gdn_in:], mlstm_i_bias[j], mlstm_f_bias[j],
                             hyb_w_out[j][:gdn_out], hyb_w_out[j][gdn_out:], ln_mix_g[l], ln_mix_b[l], alpha)
        else:
            h = _ssd_layer(h, bsz, s, ssd_w_in[j], ssd_conv_w[j], ssd_conv_b[j], ssd_a_log[j], ssd_dt_bias[j],
                           ssd_d_skip[j], ssd_norm_w[j], ssd_w_out[j], ln_mix_g[l], ln_mix_b[l], alpha)
        h = _ffn(h, *post, ln_post_g, ln_post_b, l, alpha)
    return h.reshape(bsz, s, d)
```

```python
import functools

import jax
import jax.numpy as jnp
import numpy as np
from jax import lax
from jax.experimental import pallas as pl
from jax.experimental.pallas import tpu as pltpu

F32 = jnp.float32
BF16 = jnp.bfloat16

LN_EPS = 1e-5
RMS_EPS = 1e-6
L2_EPS = 1e-6
CHUNK = 64
CONV_K = 4
NEG = -1e30

GDN_HEADS, GDN_DK, GDN_DV = 8, 64, 64
MLSTM_HEADS, MLSTM_DK, MLSTM_DV = 4, 64, 128
SSD_HEADDIM, SSD_GROUPS, SSD_STATE = 64, 4, 128

LANES = 128
SUBLANES = 8
VMEM_LIMIT = 56 * 1024 * 1024
GROUP = 4 * CHUNK

TM_DENSE = 512
TT_MIX = 256
TT_WIDE = 512


def _cparams(*sem):
    return pltpu.CompilerParams(dimension_semantics=sem, vmem_limit_bytes=VMEM_LIMIT)


def _dot(a, b):
    return jnp.dot(a, b, preferred_element_type=F32)


def _dot_nt(a, b):
    return lax.dot_general(a, b, (((1,), (1,)), ((), ())), preferred_element_type=F32)


def _dot_tn(a, b):
    return lax.dot_general(a, b, (((0,), (0,)), ((), ())), preferred_element_type=F32)


def _layer_norm(y, g, b):
    mu = jnp.mean(y, -1, keepdims=True)
    yc = y - mu
    var = jnp.mean(yc * yc, -1, keepdims=True)
    return yc * lax.rsqrt(var + LN_EPS) * g + b


def _sigmoid(x):
    return 1.0 / (1.0 + jnp.exp(-x))


def _silu(x):
    return x * _sigmoid(x)


def _softplus(x):
    return jnp.maximum(x, 0.0) + jnp.log(1.0 + jnp.exp(-jnp.abs(x)))


def _split(x, parts):
    out = []
    r = x
    for p in range(parts):
        h = r.astype(BF16)
        out.append(h)
        if p + 1 < parts:
            r = r - h.astype(F32)
    return out


def _sel_rows(m01x3_ref, x):
    rows = m01x3_ref.shape[0]
    return jnp.concatenate([_dot(m01x3_ref[...], jnp.concatenate(_split(x[r:r + rows], 3), axis=0))
                            for r in range(0, x.shape[0], rows)], axis=0)


def _sel_cols(x, m01x3_ref, parts=3):
    return _dot(jnp.concatenate(_split(x, parts), axis=1), m01x3_ref[...])


def _head_sums(x, head_width):
    low = lax.broadcasted_iota(jnp.int32, (1, LANES), 1) < head_width
    out = []
    for start in range(0, x.shape[1], LANES):
        blk = x[:, start:start + LANES]
        s_low = jnp.sum(jnp.where(low, blk, 0.0), axis=-1, keepdims=True)
        s_high = jnp.sum(jnp.where(low, 0.0, blk), axis=-1, keepdims=True)
        out.append(jnp.where(low, s_low, s_high))
    return jnp.concatenate(out, axis=1)


def _tile4(x):
    return jnp.concatenate([x, x, x, x], axis=0)


def _block_diag(x, mask_bf16):
    return _tile4(x.astype(BF16)) * mask_bf16


def _ffn_kernel(x_ref, wg_ref, wu_ref, wd_ref, g_ref, b_ref, o_ref, *, fc, alpha):
    x = x_ref[...]
    xb = x.astype(BF16)
    acc = jnp.zeros_like(x)
    for c in range(wg_ref.shape[1] // fc):
        sl = slice(c * fc, (c + 1) * fc)
        hg = _dot(xb, wg_ref[:, sl].astype(BF16))
        hu = _dot(xb, wu_ref[:, sl].astype(BF16))
        acc = acc + _dot((_silu(hg) * hu).astype(BF16), wd_ref[sl, :].astype(BF16))
    o_ref[...] = _layer_norm(alpha * x + 0.5 * acc, g_ref[...], b_ref[...])


def _const_spec(shape, grid_rank=1):
    zeros = (0,) * len(shape)
    if grid_rank == 1:
        return pl.BlockSpec(shape, lambda i: zeros, pipeline_mode=pl.Buffered(1))
    return pl.BlockSpec(shape, lambda b, i: zeros, pipeline_mode=pl.Buffered(1))


def _ffn(x, wg, wu, wd, g, b, layer, alpha):
    t, d = x.shape
    f = wg.shape[2]
    tm = TM_DENSE

    def layer_spec(rows, cols):
        return pl.BlockSpec((None, rows, cols), lambda i: (layer, 0, 0), pipeline_mode=pl.Buffered(1))

    return pl.pallas_call(
        functools.partial(_ffn_kernel, fc=2 * LANES, alpha=alpha),
        grid=(t // tm,),
        in_specs=[pl.BlockSpec((tm, d), lambda i: (i, 0)),
                  layer_spec(d, f), layer_spec(d, f), layer_spec(f, d),
                  _const_spec((1, d)), _const_spec((1, d))],
        out_specs=pl.BlockSpec((tm, d), lambda i: (i, 0)),
        out_shape=jax.ShapeDtypeStruct((t, d), F32),
        compiler_params=_cparams("parallel"),
        name="ffn",
    )(x, wg, wu, wd, g[layer].reshape(1, d), b[layer].reshape(1, d))


def _chunk_tri(tt, inclusive_lower):
    i = np.arange(tt)[:, None]
    j = np.arange(tt)[None, :]
    same = (i // CHUNK) == (j // CHUNK)
    m = same & (j <= i) if inclusive_lower else same
    return jnp.asarray(np.tile(m.astype(np.float32), (1, 3)), BF16)


def _expand_mat(n_heads, width, parts=3):
    e = np.zeros((LANES, n_heads * width), np.float32)
    for h in range(n_heads):
        e[h, h * width:(h + 1) * width] = 1.0
    return jnp.asarray(np.tile(e, (parts, 1)), BF16)


def _wide_masks():
    i = np.arange(CHUNK)[:, None]
    j = (np.arange(GROUP) % CHUNK)[None, :]
    ident = i == j
    causal = i >= j
    strict = i > j
    d8 = (i // 8) == (j // 8)
    o16 = ((i // 16) == (j // 16)) & ((i // 8) == (j // 8) + 1)
    o32 = ((i // 32) == (j // 32)) & ((i // 16) == (j // 16) + 1)
    o64 = (i // 32) == (j // 32) + 1
    return jnp.asarray(np.stack([ident, causal, strict, d8, o16, o32, o64]).astype(np.float32))


M_IDENT, M_CAUSAL, M_STRICT, M_D8, M_O16, M_O32, M_O64 = range(7)


def _bd_mask(n_cols, col_block):
    r = np.arange(GROUP)[:, None] // CHUNK
    c = (np.arange(n_cols)[None, :] % (4 * col_block)) // col_block
    return jnp.asarray((r == c).astype(np.float32))


def _head_masks(n_heads, width):
    m = np.zeros((SUBLANES, n_heads * width), np.float32)
    for h in range(n_heads):
        m[h, h * width:(h + 1) * width] = 1.0
    return jnp.asarray(m)


def _pad_cols(w):
    return jnp.pad(w, ((0, 0), (0, LANES - w.shape[1])))


def _pad_lanes(v):
    v = v.reshape(1, -1).astype(F32)
    return jnp.pad(v, ((0, 0), (0, LANES - v.shape[1])))


def _project_conv_silu(xb, w_ref, col0, halo, cw_ref, cb_ref, dst, tt, width, n_seq=1):
    for start in range(0, width, 4 * LANES):
        sl = slice(start, min(start + 4 * LANES, width))
        p_all = _dot(xb, w_ref[:, col0 + sl.start:col0 + sl.stop])
        for s in range(n_seq):
            p = p_all[s * tt:(s + 1) * tt]
            hrows = slice(s * SUBLANES, (s + 1) * SUBLANES)
            full = jnp.concatenate([halo[hrows, sl], p], axis=0)
            acc = cw_ref[CONV_K - 1:CONV_K, sl] * p
            for k in range(CONV_K - 1):
                acc = acc + cw_ref[k:k + 1, sl] * pltpu.roll(full, CONV_K - 1 - k, axis=0)[SUBLANES:SUBLANES + tt]
            if cb_ref is not None:
                acc = acc + cb_ref[:, sl]
            dst[s * tt:(s + 1) * tt, sl] = _silu(acc)
            halo[hrows, sl] = p[tt - SUBLANES:tt]


def _tile_spec(tt, width, nt):
    return pl.BlockSpec((tt, width), lambda b, i: (b * nt + i, 0))


def _ssd_kernel(h_ref, win_ref, wout_ref, lg_ref, lb_ref,
                cw_ref, cb_ref, alog_ref, dtb_ref, dskip_ref, nw_ref, tri_ref, e_ref, wm_ref, mbd_ref,
                o_ref, zbuf, halo, xc, acol, xdt, hst, *, alpha):
    tt = h_ref.shape[0]
    ncx = zbuf.shape[1]
    ncb = halo.shape[1] - ncx
    gw = ncx // SSD_GROUPS

    @pl.when(pl.program_id(1) == 0)
    def _():
        halo[...] = jnp.zeros_like(halo)
        hst[...] = jnp.zeros_like(hst)

    hb = h_ref[...].astype(BF16)
    _project_conv_silu(hb, win_ref, ncx, halo, cw_ref, cb_ref, xc, tt, ncx + ncb)

    dt = _softplus(_dot(hb, win_ref[:, 2 * ncx + ncb:]) + dtb_ref[...])
    a_cum = _sel_rows(tri_ref, dt * (-jnp.exp(alog_ref[...])))
    for start in range(0, ncx, 4 * LANES):
        sl = slice(start, start + 4 * LANES)
        zbuf[:, sl] = _silu(_dot(hb, win_ref[:, sl]))
        acol[:, sl] = _sel_cols(a_cum, e_ref.at[:, sl])
        xdt[:, sl] = xc[:, sl] * _sel_cols(dt, e_ref.at[0:2 * LANES, sl], 2)

    ident = wm_ref[M_IDENT]
    causal = wm_ref[M_CAUSAL] > 0.5
    mbd = mbd_ref[...].astype(BF16)
    groups = range(SSD_GROUPS)
    halves = range(gw // GROUP)

    def chunk(c, carry):
        r0 = pl.multiple_of(c * CHUNK, CHUNK)
        rows = pl.ds(r0, CHUNK)
        gls = [slice(g * gw, (g + 1) * gw) for g in groups]
        b_gs = [xc[rows, ncx + g * SSD_STATE:ncx + (g + 1) * SSD_STATE].astype(BF16) for g in groups]
        c_gs = [xc[rows, ncx + ncb // 2 + g * SSD_STATE:ncx + ncb // 2 + (g + 1) * SSD_STATE].astype(BF16)
                for g in groups]
        cb_wide = [_dot_nt(c_gs[g], _tile4(b_gs[g])) for g in groups]
        a_gs = [acol[rows, gl] for gl in gls]
        a_lasts = [acol[pl.ds(r0 + CHUNK - 1, 1), gl] for gl in gls]
        x_gs = [xdt[rows, gl] for gl in gls]
        lhs, rhs = [], []
        for g in groups:
            for hh in halves:
                hl = slice(hh * GROUP, (hh + 1) * GROUP)
                a_c = a_gs[g][:, hl]
                a_row = jnp.sum(a_c * ident, axis=0, keepdims=True)
                seg = jnp.exp(jnp.where(causal, a_c - a_row, NEG))
                lhs.append((cb_wide[g] * seg).astype(BF16))
                rhs.append(_block_diag(x_gs[g][:, hl], mbd))
        y_diag = [_dot(a, b) for a, b in zip(lhs, rhs)]
        h_prev = [hst[g] for g in groups]
        y_off = [_dot(c_gs[g], h_prev[g].astype(BF16)) for g in groups]
        upd = [_dot_tn(b_gs[g], (x_gs[g] * jnp.exp(a_lasts[g] - a_gs[g])).astype(BF16)) for g in groups]
        for g in groups:
            gl = gls[g]
            hst[g] = h_prev[g] * jnp.exp(a_lasts[g]) + upd[g]
            y = jnp.concatenate(y_diag[g * len(halves):(g + 1) * len(halves)], axis=1)
            y = y + y_off[g] * jnp.exp(a_gs[g]) + dskip_ref[:, gl] * xc[rows, gl]
            y = y * zbuf[rows, gl]
            xdt[rows, gl] = y * lax.rsqrt(jnp.mean(y * y, -1, keepdims=True) + RMS_EPS) * nw_ref[:, gl]
        return carry

    lax.fori_loop(0, tt // CHUNK, chunk, 0, unroll=True)
    acc = alpha * h_ref[...] + _dot(xdt[...].astype(BF16), wout_ref[...])
    o_ref[...] = _layer_norm(acc, lg_ref[...], lb_ref[...])


def _ssd_layer(h, bsz, s, w_in, conv_w, conv_b, a_log, dt_bias, d_skip, norm_w, w_out, ln_g, ln_b, alpha):
    tt = TT_MIX
    nt = s // tt
    d = h.shape[1]
    n_heads = a_log.shape[0]
    ncx = n_heads * SSD_HEADDIM
    ncb = 2 * SSD_GROUPS * SSD_STATE
    w_in = jnp.concatenate([w_in[:, :2 * ncx + ncb], _pad_cols(w_in[:, 2 * ncx + ncb:])], axis=1).astype(BF16)
    consts = [w_in, w_out.astype(BF16), ln_g.reshape(1, d), ln_b.reshape(1, d),
              conv_w.astype(F32), conv_b.reshape(1, -1).astype(F32), _pad_lanes(a_log), _pad_lanes(dt_bias),
              jnp.repeat(d_skip.astype(F32), SSD_HEADDIM).reshape(1, ncx), norm_w.reshape(1, ncx).astype(F32),
              _chunk_tri(tt, True), _expand_mat(n_heads, SSD_HEADDIM), _wide_masks(), _bd_mask(GROUP, CHUNK)]
    return pl.pallas_call(
        functools.partial(_ssd_kernel, alpha=alpha),
        grid=(bsz, nt),
        in_specs=[_tile_spec(tt, d, nt)] + [_const_spec(c.shape, 2) for c in consts],
        out_specs=_tile_spec(tt, d, nt),
        out_shape=jax.ShapeDtypeStruct((bsz * s, d), F32),
        scratch_shapes=[pltpu.VMEM((tt, ncx), F32),
                        pltpu.VMEM((SUBLANES, ncx + ncb), F32), pltpu.VMEM((tt, ncx + ncb), F32),
                        pltpu.VMEM((tt, ncx), F32), pltpu.VMEM((tt, ncx), F32),
                        pltpu.VMEM((SSD_GROUPS, SSD_STATE, ncx // SSD_GROUPS), F32)],
        compiler_params=_cparams("parallel", "arbitrary"),
        name="ssd_layer",
    )(h, *consts)


def _per_head(x, hm_in, hm_outs, reduce_fn, fill):
    outs = [None] * len(hm_outs)
    for h in range(MLSTM_HEADS):
        r = reduce_fn(jnp.where(hm_in[h:h + 1, :] > 0.5, x, fill), axis=-1, keepdims=True)
        outs = [jnp.broadcast_to(r, (x.shape[0], hm.shape[1])) if o is None
                else jnp.where(hm[h:h + 1, :] > 0.5, r, o) for o, hm in zip(outs, hm_outs)]
    return outs


def _mlstm_kernel(h_ref, oa_ref, win_ref, woa_ref, wob_ref, lg_ref, lb_ref, ib_ref, fb_ref,
                  tri_ref, e2_ref, e5_ref, wm_ref, mbd_ref, mbdv_ref, hm2_ref, hm5_ref,
                  o_ref, qkvo, ob, b2, li2, b5, li5, cst, nst, m5s, m2s, *, alpha):
    tt = h_ref.shape[0]
    kw = MLSTM_HEADS * MLSTM_DK
    vw = MLSTM_HEADS * MLSTM_DV
    scale = MLSTM_DK ** -0.5

    @pl.when(pl.program_id(1) == 0)
    def _():
        cst[...] = jnp.zeros_like(cst)
        nst[...] = jnp.zeros_like(nst)
        m5s[...] = jnp.zeros_like(m5s)
        m2s[...] = jnp.zeros_like(m2s)

    hb = h_ref[...].astype(BF16)
    gates = _dot(hb, win_ref[:, 2 * kw + 2 * vw:])
    log_i = gates[:, 0:LANES] + ib_ref[...]
    log_f = -_softplus(-(gates[:, LANES:] + fb_ref[...]))
    for start in range(0, 2 * kw + 2 * vw, 4 * LANES):
        sl = slice(start, start + 4 * LANES)
        qkvo[:, sl] = _dot(hb, win_ref[:, sl])

    b = _sel_rows(tri_ref, log_f)
    for src, d2, d5 in ((b, b2, b5), (log_i, li2, li5)):
        d2[...] = _sel_cols(src, e2_ref)
        d5[...] = _sel_cols(src, e5_ref)

    ident = wm_ref[M_IDENT]
    causal = wm_ref[M_CAUSAL] > 0.5
    mbd = mbd_ref[...].astype(BF16)
    mbdv_f = mbdv_ref[...]
    mbdv = mbdv_f.astype(BF16)
    hm2 = hm2_ref[...]
    hm5 = hm5_ref[...]
    chunks = [slice(c * CHUNK, (c + 1) * CHUNK) for c in range(tt // CHUNK)]

    qs = [qkvo[rows, 0:kw] for rows in chunks]
    ks = [qkvo[rows, kw:2 * kw] * scale for rows in chunks]
    vs = [qkvo[rows, 2 * kw:2 * kw + vw] for rows in chunks]
    d_logs, mi2s, mi5s, me2s, me5s, k_ends = [], [], [], [], [], []
    for rows, k in zip(chunks, ks):
        bc2, lic2 = b2[rows, :], li2[rows, :]
        row_v = jnp.sum((lic2 - bc2) * ident, axis=0, keepdims=True)
        d_log = jnp.where(causal, bc2 + row_v, NEG)
        mi2, mi5 = _per_head(d_log, hm2, (hm2, hm5), jnp.max, NEG)
        d_logs.append(d_log)
        mi2s.append(mi2)
        mi5s.append(mi5)
        bl2, bl5 = b2[rows.stop - 1:rows.stop, :], b5[rows.stop - 1:rows.stop, :]
        w2 = bl2 - bc2 + lic2
        me2 = jnp.max(w2, axis=0, keepdims=True)
        me2s.append(me2)
        me5s.append(jnp.max(bl5 - b5[rows, :] + li5[rows, :], axis=0, keepdims=True))
        k_ends.append(k * jnp.exp(w2 - me2))
    s_qks = [_dot_nt(q.astype(BF16), _block_diag(k, mbd)) * jnp.exp(d_log - mi2)
             for q, k, d_log, mi2 in zip(qs, ks, d_logs, mi2s)]
    c_ends = [_dot_tn(k_end.astype(BF16), v.astype(BF16)) * mbdv_f for k_end, v in zip(k_ends, vs)]
    num_is = [_dot(s_qk.astype(BF16), _block_diag(v, mbdv)) for s_qk, v in zip(s_qks, vs)]
    den_is = [_per_head(s_qk, hm2, (hm5,), jnp.sum, 0.0)[0] for s_qk in s_qks]

    c_st, n_st, m5, m2 = cst[...], nst[...], m5s[...], m2s[...]
    c_sts, n_sts, m5l = [], [], []
    for rows, c_end, k_end, me5, me2 in zip(chunks, c_ends, k_ends, me5s, me2s):
        c_sts.append(c_st)
        n_sts.append(n_st)
        m5l.append(m5)
        bl5, bl2 = b5[rows.stop - 1:rows.stop, :], b2[rows.stop - 1:rows.stop, :]
        mn5 = jnp.maximum(bl5 + m5, me5)
        mn2 = jnp.maximum(bl2 + m2, me2)
        c_st = c_st * jnp.exp(bl5 + m5 - mn5) + c_end * jnp.exp(me5 - mn5)
        n_st = n_st * jnp.exp(bl2 + m2 - mn2) + jnp.sum(k_end, axis=0, keepdims=True) * jnp.exp(me2 - mn2)
        m5, m2 = mn5, mn2
    cst[...], nst[...], m5s[...], m2s[...] = c_st, n_st, m5, m2
    q_cs = [_dot(q.astype(BF16), c.astype(BF16)) for q, c in zip(qs, c_sts)]
    for rows, q, q_c, n, m, mi5, num_i, den_i in zip(chunks, qs, q_cs, n_sts, m5l, mi5s, num_is, den_is):
        a5 = b5[rows, :] + m
        m_t = jnp.maximum(a5, mi5)
        ea, ei = jnp.exp(a5 - m_t), jnp.exp(mi5 - m_t)
        (qn,) = _per_head(q * n, hm2, (hm5,), jnp.sum, 0.0)
        h = (ea * q_c + ei * num_i) / jnp.maximum(jnp.abs(ea * qn + ei * den_i), jnp.exp(-m_t))
        ob[rows, :] = _sigmoid(qkvo[rows, 2 * kw + vw:]) * h

    acc = (alpha * h_ref[...] + _dot(oa_ref[...].astype(BF16), woa_ref[...])
           + _dot(ob[...].astype(BF16), wob_ref[...]))
    o_ref[...] = _layer_norm(acc, lg_ref[...], lb_ref[...])


def _mlstm_layer(h, o_a, bsz, s, w_in, i_bias, f_bias, w_out_a, w_out_b, ln_g, ln_b, alpha):
    tt = TT_WIDE
    nt = s // tt
    d = h.shape[1]
    kw = MLSTM_HEADS * MLSTM_DK
    vw = MLSTM_HEADS * MLSTM_DV
    main = 2 * kw + 2 * vw
    w_in = jnp.concatenate([w_in[:, :main], _pad_cols(w_in[:, main:main + MLSTM_HEADS]),
                            _pad_cols(w_in[:, main + MLSTM_HEADS:])], axis=1).astype(BF16)
    consts = [w_in, w_out_a.astype(BF16), w_out_b.astype(BF16), ln_g.reshape(1, d), ln_b.reshape(1, d),
              _pad_lanes(i_bias), _pad_lanes(f_bias), _chunk_tri(TT_MIX, True),
              _expand_mat(MLSTM_HEADS, MLSTM_DK), _expand_mat(MLSTM_HEADS, MLSTM_DV), _wide_masks(),
              _bd_mask(kw, MLSTM_DK), _bd_mask(vw, MLSTM_DV),
              _head_masks(MLSTM_HEADS, MLSTM_DK), _head_masks(MLSTM_HEADS, MLSTM_DV)]
    return pl.pallas_call(
        functools.partial(_mlstm_kernel, alpha=alpha),
        grid=(bsz, nt),
        in_specs=[_tile_spec(tt, d, nt), _tile_spec(tt, o_a.shape[1], nt)]
                 + [_const_spec(c.shape, 2) for c in consts],
        out_specs=_tile_spec(tt, d, nt),
        out_shape=jax.ShapeDtypeStruct((bsz * s, d), F32),
        scratch_shapes=[pltpu.VMEM((tt, main), F32), pltpu.VMEM((tt, vw), F32)]
                       + [pltpu.VMEM((tt, kw), F32)] * 2 + [pltpu.VMEM((tt, vw), F32)] * 2
                       + [pltpu.VMEM((kw, vw), F32), pltpu.VMEM((1, kw), F32),
                          pltpu.VMEM((1, vw), F32), pltpu.VMEM((1, kw), F32)],
        compiler_params=_cparams("parallel", "arbitrary"),
        name="mlstm_layer",
    )(h, o_a, *consts)


def _unit_lower_inverse(ls, wm_ref, mbd):
    def mm(xs, ys):
        return [_dot(a.astype(BF16), _block_diag(b, mbd)) for a, b in zip(xs, ys)]

    ident = wm_ref[M_IDENT]
    d8 = wm_ref[M_D8]
    l8 = [l * d8 for l in ls]
    l8_2 = mm(l8, l8)
    l8_4 = mm(l8_2, l8_2)
    t = mm(mm([ident - a for a in l8], [ident + a for a in l8_2]), [ident + a for a in l8_4])
    for level in (M_O16, M_O32, M_O64):
        off = wm_ref[level]
        corr = mm(mm(t, [l * off for l in ls]), t)
        t = [a - b for a, b in zip(t, corr)]
    return t


def _gdn_kernel(h_ref, win_ref, cw_ref, alog_ref, dtb_ref, nw_ref,
                tri_ref, e_ref, wm_ref, mbd_ref, mbd2_ref,
                o_ref, zbuf, halo, xc, gcw, btw, u0b, wdb, atb, qdb, kdb, ob, sst):
    n_seq, tt_seq = h_ref.shape[0], h_ref.shape[1]
    tt = n_seq * tt_seq
    hw = GDN_HEADS * GDN_DK
    n_groups = hw // GROUP

    @pl.when(pl.program_id(1) == 0)
    def _():
        halo[...] = jnp.zeros_like(halo)
        sst[...] = jnp.zeros_like(sst)

    hb = jnp.concatenate([h_ref[s] for s in range(n_seq)], axis=0).astype(BF16)
    _project_conv_silu(hb, win_ref, 0, halo, cw_ref, None, xc, tt_seq, 3 * hw, n_seq)
    gates = _dot(hb, win_ref[:, 4 * hw:])
    beta = _sigmoid(gates[:, 0:LANES])
    g = -jnp.exp(alog_ref[...]) * _softplus(gates[:, LANES:] + dtb_ref[...])
    zbuf[...] = _dot(hb, win_ref[:, 3 * hw:4 * hw])

    for j, s in ((0, GDN_DK ** -0.5), (1, 1.0)):
        sl = slice(j * hw, (j + 1) * hw)
        x = xc[:, sl]
        xc[:, sl] = x * (lax.rsqrt(_head_sums(x * x, GDN_DK) + L2_EPS) * s)

    gcw[...] = _sel_cols(_sel_rows(tri_ref, g), e_ref)
    btw[...] = _sel_cols(beta, e_ref)

    ident = wm_ref[M_IDENT]
    causal = wm_ref[M_CAUSAL] > 0.5
    strict = wm_ref[M_STRICT]
    mbd_f = mbd_ref[...]
    mbd = mbd_f.astype(BF16)
    mbd2 = mbd2_ref[...].astype(BF16)
    groups = range(n_groups)
    lanes = [slice(grp * GROUP, (grp + 1) * GROUP) for grp in groups]

    tiles = [(slice(c * CHUNK, (c + 1) * CHUNK), grp) for c in range(tt // CHUNK) for grp in groups]
    kbs, kbds, decs, lowers = [], [], [], []
    for rows, grp in tiles:
        l = lanes[grp]
        gc = gcw[rows, l]
        k = xc[rows, hw + grp * GROUP:hw + (grp + 1) * GROUP]
        g_row = jnp.sum(gc * ident, axis=0, keepdims=True)
        decs.append(jnp.exp(jnp.where(causal, gc - g_row, NEG)))
        kbs.append(k * btw[rows, l])
        kbds.append(_block_diag(k, mbd))
        kdb[rows, l] = k * jnp.exp(gc[CHUNK - 1:CHUNK, :] - gc)
    for kb, kbd, dec in zip(kbs, kbds, decs):
        lowers.append(_dot_nt(kb.astype(BF16), kbd) * dec * strict)
    for (rows, grp), kbd, dec in zip(tiles, kbds, decs):
        q = xc[rows, lanes[grp]]
        atb[rows, lanes[grp]] = _dot_nt(q.astype(BF16), kbd) * dec
    t_invs = _unit_lower_inverse(lowers, wm_ref, mbd)
    rbds = []
    for (rows, grp), kb in zip(tiles, kbs):
        l = lanes[grp]
        eg = jnp.exp(gcw[rows, l])
        v = xc[rows, 2 * hw + grp * GROUP:2 * hw + (grp + 1) * GROUP]
        rbds.append(_block_diag(jnp.concatenate([v * btw[rows, l], kb * eg], axis=1), mbd2))
        qdb[rows, l] = xc[rows, l] * eg
    for (rows, grp), t_inv, rbd in zip(tiles, t_invs, rbds):
        sol = _dot(t_inv.astype(BF16), rbd)
        u0b[rows, lanes[grp]] = sol[:, :GROUP]
        wdb[rows, lanes[grp]] = sol[:, GROUP:]

    chains = [(s, grp) for s in range(n_seq) for grp in groups]

    def chunk(c, carry):
        r0s = [pl.multiple_of(s * tt_seq + c * CHUNK, CHUNK) for s in range(n_seq)]
        rows = [pl.ds(r0s[s], CHUNK) for s, _ in chains]
        ls = [lanes[grp] for _, grp in chains]
        sts = [sst[s * n_groups + grp] for s, grp in chains]
        rs = [_dot(jnp.concatenate([wdb[r, l], qdb[r, l]], axis=0).astype(BF16), st.astype(BF16))
              for r, l, st in zip(rows, ls, sts)]
        us = [u0b[r, l] - x[:CHUNK] for r, l, x in zip(rows, ls, rs)]
        intra = [_dot(atb[r, l].astype(BF16), _block_diag(u, mbd)) for r, l, u in zip(rows, ls, us)]
        upd = [_dot_tn(kdb[r, l].astype(BF16), u.astype(BF16)) for r, l, u in zip(rows, ls, us)]
        for n, (s, grp) in enumerate(chains):
            ob[rows[n], ls[n]] = rs[n][CHUNK:] + intra[n]
            g_last = gcw[pl.ds(r0s[s] + CHUNK - 1, 1), ls[n]]
            sst[s * n_groups + grp] = sts[n] * jnp.exp(g_last) + upd[n] * mbd_f
        return carry

    lax.fori_loop(0, tt_seq // CHUNK, chunk, 0, unroll=True)

    o = ob[...]
    ms = _head_sums(o * o, GDN_DV) * (1.0 / GDN_DV)
    o = o * lax.rsqrt(ms + RMS_EPS) * nw_ref[...] * _silu(zbuf[...])
    for s in range(n_seq):
        o_ref[s] = o[s * tt_seq:(s + 1) * tt_seq]


def _gdn_mixer(h, bsz, s, w_in, conv_w, a_log, dt_bias, norm_w):
    tt = TT_MIX
    n_seq = 2
    nt = s // tt
    d = h.shape[1]
    hw = GDN_HEADS * GDN_DK
    w_in = jnp.concatenate([w_in[:, :4 * hw], _pad_cols(w_in[:, 4 * hw:4 * hw + GDN_HEADS]),
                            _pad_cols(w_in[:, 4 * hw + GDN_HEADS:])], axis=1).astype(BF16)
    consts = [w_in, conv_w.astype(F32), _pad_lanes(a_log), _pad_lanes(dt_bias),
              jnp.tile(norm_w.astype(F32), GDN_HEADS).reshape(1, hw),
              _chunk_tri(tt, True), _expand_mat(GDN_HEADS, GDN_DK), _wide_masks(),
              _bd_mask(GROUP, CHUNK), _bd_mask(2 * GROUP, CHUNK)]
    rows = n_seq * tt
    out = pl.pallas_call(
        _gdn_kernel,
        grid=(bsz // n_seq, nt),
        in_specs=[pl.BlockSpec((n_seq, tt, d), lambda b, i: (b, i, 0))] + [_const_spec(c.shape, 2) for c in consts],
        out_specs=pl.BlockSpec((n_seq, tt, hw), lambda b, i: (b, i, 0)),
        out_shape=jax.ShapeDtypeStruct((bsz, s, hw), F32),
        scratch_shapes=[pltpu.VMEM((rows, hw), F32),
                        pltpu.VMEM((n_seq * SUBLANES, 3 * hw), F32), pltpu.VMEM((rows, 3 * hw), F32)]
                       + [pltpu.VMEM((rows, hw), F32)] * 8
                       + [pltpu.VMEM((n_seq * hw // GROUP, GROUP, GROUP), F32)],
        compiler_params=_cparams("parallel", "arbitrary"),
        name="gdn_mixer",
    )(h.reshape(bsz, s, d), *consts)
    return out.reshape(bsz * s, hw)


def kernel(x, ffn_pre_w_gate, ffn_pre_w_up, ffn_pre_w_down, ln_pre_g, ln_pre_b, hyb_w_in, hyb_conv_w, gdn_a_log, gdn_dt_bias, gdn_norm_w, mlstm_i_bias, mlstm_f_bias, hyb_w_out, ssd_w_in, ssd_conv_w, ssd_conv_b, ssd_a_log, ssd_dt_bias, ssd_d_skip, ssd_norm_w, ssd_w_out, ln_mix_g, ln_mix_b, ffn_post_w_gate, ffn_post_w_up, ffn_post_w_down, ln_post_g, ln_post_b):
    bsz, s, d = x.shape
    depth = ffn_pre_w_gate.shape[0]
    alpha = (2 * depth) ** 0.25
    gdn_in = 4 * GDN_HEADS * GDN_DK + 2 * GDN_HEADS
    gdn_out = GDN_HEADS * GDN_DV
    pre = (ffn_pre_w_gate, ffn_pre_w_up, ffn_pre_w_down)
    post = (ffn_post_w_gate, ffn_post_w_up, ffn_post_w_down)
    h = x.reshape(bsz * s, d)
    for l in range(depth):
        h = _ffn(h, *pre, ln_pre_g, ln_pre_b, l, alpha)
        j = l // 2
        if l % 2 == 0:
            o_a = _gdn_mixer(h, bsz, s, hyb_w_in[j][:, :gdn_in], hyb_conv_w[j], gdn_a_log[j], gdn_dt_bias[j],
                             gdn_norm_w[j])
            h = _mlstm_layer(h, o_a, bsz, s, hyb_w_in[j][:, gdn_in:], mlstm_i_bias[j], mlstm_f_bias[j],
                             hyb_w_out[j][:gdn_out], hyb_w_out[j][gdn_out:], ln_mix_g[l], ln_mix_b[l], alpha)
        else:
            h = _ssd_layer(h, bsz, s, ssd_w_in[j], ssd_conv_w[j], ssd_conv_b[j], ssd_a_log[j], ssd_dt_bias[j],
                           ssd_d_skip[j], ssd_norm_w[j], ssd_w_out[j], ln_mix_g[l], ln_mix_b[l], alpha)
        h = _ffn(h, *post, ln_post_g, ln_post_b, l, alpha)
    return h.reshape(bsz, s, d)
```
